```python
import jax, jax.numpy as jnp
from jax import lax
import numpy as np

D_MODEL = 2048
BATCH = 4
SEQ = 2048
DEPTH = 4

CTX_LEN = 256
GRID_W = 64
Q_BLOCK = 128
ROPE_THETA = 10000.0
EPS = 1e-6

GQA_HEADS = 8
GQA_KV_HEADS = 2
GQA_GROUP = GQA_HEADS // GQA_KV_HEADS
GQA_HEAD_DIM = 128
GQA_WIDTH = GQA_HEADS * GQA_HEAD_DIM
MLA_HEADS = 8
MLA_NOPE_DIM = 128
MLA_ROPE_DIM = 64
MLA_QK_DIM = MLA_NOPE_DIM + MLA_ROPE_DIM
MLA_V_DIM = 128
MLA_Q_RANK = 512
MLA_KV_RANK = 512
MLA_WIDTH = MLA_HEADS * MLA_V_DIM
D_MIX = GQA_WIDTH + MLA_WIDTH
IN_SIZES = (GQA_WIDTH, GQA_KV_HEADS * GQA_HEAD_DIM, GQA_KV_HEADS * GQA_HEAD_DIM, GQA_WIDTH,
            MLA_Q_RANK, MLA_KV_RANK, MLA_ROPE_DIM, MLA_WIDTH)
D_IN = GQA_WIDTH + 2 * GQA_KV_HEADS * GQA_HEAD_DIM + GQA_WIDTH + MLA_Q_RANK + MLA_KV_RANK + MLA_ROPE_DIM + MLA_WIDTH

kernel_name = "hybrid_gqa_mla_prefix_dit"


def rms_norm(x, g):
    xf = x.astype(jnp.float32)
    y = xf * lax.rsqrt(jnp.mean(xf * xf, axis=-1, keepdims=True) + EPS)
    return (y * g.astype(jnp.float32)).astype(x.dtype)


def axial_rope_tables(n_tokens, rot_dim, dtype):
    rows = n_tokens // GRID_W
    row = jnp.repeat(jnp.arange(rows, dtype=jnp.float32), GRID_W)
    col = jnp.tile(jnp.arange(GRID_W, dtype=jnp.float32), rows)
    n_freq = rot_dim // 4
    inv = ROPE_THETA ** (-jnp.arange(n_freq, dtype=jnp.float32) / n_freq)
    ang = jnp.concatenate([row[:, None] * inv[None], col[:, None] * inv[None]], axis=-1)
    return jnp.cos(ang).astype(dtype), jnp.sin(ang).astype(dtype)


def apply_rope(x, tables):
    cos, sin = tables
    cos, sin = cos[None, :, None, :], sin[None, :, None, :]
    x1, x2 = jnp.split(x, 2, axis=-1)
    return jnp.concatenate([x1 * cos - x2 * sin, x2 * cos + x1 * sin], axis=-1)


def block_attention(q, k, v):
    B, Sq, Hk, G, Dk = q.shape
    nb = Sq // Q_BLOCK
    scale = Dk ** -0.5
    qb = q.reshape(B, nb, Q_BLOCK, Hk, G, Dk).swapaxes(0, 1)

    def one_block(qblk):
        s = jnp.einsum('bqhgd,bshd->bhgqs', qblk, k).astype(jnp.float32) * scale
        p = jax.nn.softmax(s, axis=-1).astype(v.dtype)
        return jnp.einsum('bhgqs,bshd->bqhgd', p, v)

    o = lax.map(one_block, qb)
    return o.swapaxes(0, 1).reshape(B, Sq, Hk * G * v.shape[-1])


def mixer_inputs(h, w_in, q_gain, k_gain, cq_gain, ckv_gain, w_uq, w_ukv, rope_gqa, rope_mla):
    B, S, _ = h.shape
    p = h @ w_in
    offsets = [int(o) for o in np.cumsum(IN_SIZES)[:-1]]
    q, k, v, g_gqa, cq, ckv, kr, g_mla = jnp.split(p, offsets, axis=-1)
    q = rms_norm(q.reshape(B, S, GQA_HEADS, GQA_HEAD_DIM), q_gain)
    k = rms_norm(k.reshape(B, S, GQA_KV_HEADS, GQA_HEAD_DIM), k_gain)
    v = v.reshape(B, S, GQA_KV_HEADS, GQA_HEAD_DIM)
    qm = (rms_norm(cq, cq_gain) @ w_uq).reshape(B, S, MLA_HEADS, MLA_QK_DIM)
    q_nope, q_rope = qm[..., :MLA_NOPE_DIM], qm[..., MLA_NOPE_DIM:]
    kvm = (rms_norm(ckv, ckv_gain) @ w_ukv).reshape(B, S, MLA_HEADS, MLA_NOPE_DIM + MLA_V_DIM)
    k_nope, v_mla = kvm[..., :MLA_NOPE_DIM], kvm[..., MLA_NOPE_DIM:]
    kr = kr.reshape(B, S, 1, MLA_ROPE_DIM)
    if rope_gqa is not None:
        q = apply_rope(q, rope_gqa)
        k = apply_rope(k, rope_gqa)
        q_rope = apply_rope(q_rope, rope_mla)
        kr = apply_rope(kr, rope_mla)
    q = q.reshape(B, S, GQA_KV_HEADS, GQA_GROUP, GQA_HEAD_DIM)
    q_mla = jnp.concatenate([q_nope, q_rope], axis=-1)[:, :, :, None, :]
    k_mla = jnp.concatenate([k_nope, jnp.broadcast_to(kr, (B, S, MLA_HEADS, MLA_ROPE_DIM))], axis=-1)
    return (q, k, v, g_gqa), (q_mla, k_mla, v_mla, g_mla)


def merge_groups(o_gqa, g_gqa, o_mla, g_mla, w_out):
    y = jnp.concatenate([o_gqa * jax.nn.silu(g_gqa), o_mla * jax.nn.silu(g_mla)], axis=-1)
    return y @ w_out


def setup_inputs(seed: int = 0) -> dict:
    key = jax.random.key(seed)
    ks = jax.random.split(key, 16)
    f = jnp.float32
    nrm = lambda k, shape, s: jax.random.normal(k, shape, f) * s
    return {
        "x": nrm(ks[0], (BATCH, SEQ, D_MODEL), 1.0),
        "c": nrm(ks[1], (BATCH, D_MODEL), 1.0),
        "ctx": nrm(ks[2], (BATCH, CTX_LEN, D_MODEL), 1.0),
        "c_ctx": nrm(ks[3], (D_MODEL,), 1.0),
        "w_ada": nrm(ks[4], (DEPTH, D_MODEL, 3 * D_MODEL), 0.5 * D_MODEL ** -0.5),
        "b_ada": nrm(ks[5], (DEPTH, 3 * D_MODEL), 0.01),
        "norm_g": 1.0 + nrm(ks[6], (DEPTH, D_MODEL), 0.02),
        "w_in": nrm(ks[7], (DEPTH, D_MODEL, D_IN), D_MODEL ** -0.5),
        "q_gain": 1.0 + nrm(ks[8], (DEPTH, GQA_HEAD_DIM), 0.02),
        "k_gain": 1.0 + nrm(ks[9], (DEPTH, GQA_HEAD_DIM), 0.02),
        "cq_gain": 1.0 + nrm(ks[10], (DEPTH, MLA_Q_RANK), 0.02),
        "ckv_gain": 1.0 + nrm(ks[11], (DEPTH, MLA_KV_RANK), 0.02),
        "w_uq": nrm(ks[12], (DEPTH, MLA_Q_RANK, MLA_HEADS * MLA_QK_DIM), MLA_Q_RANK ** -0.5),
        "w_ukv": nrm(ks[13], (DEPTH, MLA_KV_RANK, MLA_HEADS * (MLA_NOPE_DIM + MLA_V_DIM)), MLA_KV_RANK ** -0.5),
        "w_out": nrm(ks[14], (DEPTH, D_MIX, D_MODEL), D_MIX ** -0.5),
        "final_g": 1.0 + nrm(ks[15], (D_MODEL,), 0.02),
    }


def reference(x, c, ctx, c_ctx, w_ada, b_ada, norm_g, w_in, q_gain, k_gain, cq_gain, ckv_gain,
              w_uq, w_ukv, w_out, final_g):
    n_tok = x.shape[1]
    ROWS = n_tok // GRID_W
    rope_gqa = axial_rope_tables(ROWS * GRID_W, GQA_HEAD_DIM, x.dtype)
    rope_mla = axial_rope_tables(ROWS * GRID_W, MLA_ROPE_DIM, x.dtype)
    xc = ctx
    sc = jax.nn.silu(c)
    sc_ctx = jax.nn.silu(c_ctx)
    for l in range(DEPTH):
        shift, scale, gate = jnp.split(sc @ w_ada[l] + b_ada[l], 3, axis=-1)
        shift_c, scale_c, gate_c = jnp.split(sc_ctx @ w_ada[l] + b_ada[l], 3, axis=-1)
        h_lat = rms_norm(x, norm_g[l]) * (1.0 + scale[:, None]) + shift[:, None]
        h_ctx = rms_norm(xc, norm_g[l]) * (1.0 + scale_c) + shift_c
        lat_a, lat_b = mixer_inputs(h_lat, w_in[l], q_gain[l], k_gain[l], cq_gain[l], ckv_gain[l],
                                    w_uq[l], w_ukv[l], rope_gqa, rope_mla)
        ctx_a, ctx_b = mixer_inputs(h_ctx, w_in[l], q_gain[l], k_gain[l], cq_gain[l], ckv_gain[l],
                                    w_uq[l], w_ukv[l], None, None)
        o_a = block_attention(lat_a[0], jnp.concatenate([ctx_a[1], lat_a[1]], axis=1),
                              jnp.concatenate([ctx_a[2], lat_a[2]], axis=1))
        o_b = block_attention(lat_b[0], jnp.concatenate([ctx_b[1], lat_b[1]], axis=1),
                              jnp.concatenate([ctx_b[2], lat_b[2]], axis=1))
        x = x + gate[:, None] * merge_groups(o_a, lat_a[3], o_b, lat_b[3], w_out[l])
        if l < DEPTH - 1:
            oc_a = block_attention(ctx_a[0], ctx_a[1], ctx_a[2])
            oc_b = block_attention(ctx_b[0], ctx_b[1], ctx_b[2])
            xc = xc + gate_c * merge_groups(oc_a, ctx_a[3], oc_b, ctx_b[3], w_out[l])
    return rms_norm(x, final_g)
```

```python
import functools

import jax
import jax.numpy as jnp
import numpy as np
from jax import lax
from jax.experimental import pallas as pl
from jax.experimental.pallas import tpu as pltpu

F32 = jnp.float32
BF16 = jnp.bfloat16

D_MODEL = 2048
SEQ = 2048
CTX_LEN = 256
T_TOK = SEQ + CTX_LEN
GRID_W = 64
ROPE_THETA = 10000.0
EPS = 1e-6
DEPTH = 4

GQA_HEADS = 8
GQA_KV_HEADS = 2
GQA_GROUP = GQA_HEADS // GQA_KV_HEADS
HEAD_DIM = 128
GQA_WIDTH = GQA_HEADS * HEAD_DIM
GQA_KV_WIDTH = GQA_KV_HEADS * HEAD_DIM

MLA_HEADS = 8
MLA_NOPE = 128
MLA_ROPE = 64
MLA_QK = MLA_NOPE + MLA_ROPE
MLA_V = 128
MLA_Q_RANK = 512
MLA_KV_RANK = 512
MLA_WIDTH = MLA_HEADS * MLA_V
MLA_DK = 2 * HEAD_DIM
MLA_QK_WIDTH = MLA_HEADS * MLA_DK

IN_SIZES = (GQA_WIDTH, GQA_KV_WIDTH, GQA_KV_WIDTH, GQA_WIDTH,
            MLA_Q_RANK, MLA_KV_RANK, MLA_ROPE, MLA_WIDTH)

COL_Q = 0
COL_K = COL_Q + GQA_WIDTH
COL_V = COL_K + GQA_KV_WIDTH
COL_G = COL_V + GQA_KV_WIDTH
COL_CQ = COL_G + GQA_WIDTH + MLA_WIDTH
COL_CKV = COL_CQ + MLA_Q_RANK
COL_KR = COL_CKV + MLA_KV_RANK
D_IN_PACKED = COL_KR + HEAD_DIM

ADA_ROWS = 8
ADA_CTX_ROW = 4
ADA_TN = 1024

TM = 256
N_BLK = T_TOK // TM
N_LAT_BLK = SEQ // TM
MLA_HEADS_PER_STEP = 4

VMEM_LIMIT = 56 * 1024 * 1024


def _silu(x):
    return x / (1.0 + jnp.exp(-x))


def _rms(x):
    return x * lax.rsqrt(jnp.mean(x * x, axis=-1, keepdims=True) + EPS)


def _rot(x, cos, sin):
    return x * cos + pltpu.roll(x, HEAD_DIM // 2, 1) * sin


def _ada_kernel(c_ref, w_ref, b_ref, o_ref):
    s = _silu(c_ref[...]).astype(BF16)
    w = w_ref[...].astype(BF16)
    o_ref[...] = jnp.dot(s, w, preferred_element_type=F32) + b_ref[...]


def _ada_rows(c_all, w_ada, b_ada):
    n_out = 3 * D_MODEL
    return pl.pallas_call(
        _ada_kernel,
        grid=(DEPTH, n_out // ADA_TN),
        in_specs=[
            pl.BlockSpec((ADA_ROWS, D_MODEL), lambda l, n: (0, 0)),
            pl.BlockSpec((None, D_MODEL, ADA_TN), lambda l, n: (l, 0, n)),
            pl.BlockSpec((None, 1, ADA_TN), lambda l, n: (l, 0, n)),
        ],
        out_specs=pl.BlockSpec((None, ADA_ROWS, ADA_TN), lambda l, n: (l, 0, n)),
        out_shape=jax.ShapeDtypeStruct((DEPTH, ADA_ROWS, n_out), F32),
        compiler_params=pltpu.CompilerParams(
            dimension_semantics=("arbitrary", "arbitrary"), vmem_limit_bytes=VMEM_LIMIT),
        name="ada_rows",
    )(c_all, w_ada, b_ada.reshape(DEPTH, 1, n_out))


def _in_proj_kernel(x_ref, mod_ref, ng_ref, w_in_ref, qg_ref, kg_ref, cqg_ref, ckvg_ref,
                    w_uq_ref, w_ukv_ref, cos_a_ref, sin_a_ref, cos_m_ref, sin_m_ref,
                    qa_ref, ka_ref, va_ref, qm_ref, km_ref, vm_ref, g_ref):
    x = x_ref[...]
    h = _rms(x) * ng_ref[...] * (1.0 + mod_ref[1:2, :]) + mod_ref[0:1, :]
    p = jnp.dot(h.astype(BF16), w_in_ref[...], preferred_element_type=F32)

    cos_a, sin_a = cos_a_ref[...], sin_a_ref[...]
    cos_m, sin_m = cos_m_ref[...], sin_m_ref[...]

    q_gain = qg_ref[...] * (HEAD_DIM ** -0.5)
    for i in range(GQA_HEADS):
        c0 = COL_Q + i * HEAD_DIM
        qh = _rms(p[:, c0:c0 + HEAD_DIM]) * q_gain
        qa_ref[:, i * HEAD_DIM:(i + 1) * HEAD_DIM] = _rot(qh, cos_a, sin_a).astype(BF16)
    k_gain = kg_ref[...]
    for i in range(GQA_KV_HEADS):
        c0 = COL_K + i * HEAD_DIM
        kh = _rms(p[:, c0:c0 + HEAD_DIM]) * k_gain
        ka_ref[:, i * HEAD_DIM:(i + 1) * HEAD_DIM] = _rot(kh, cos_a, sin_a).astype(BF16)
    va_ref[...] = p[:, COL_V:COL_V + GQA_KV_WIDTH].astype(BF16)

    g_ref[...] = _silu(p[:, COL_G:COL_G + GQA_WIDTH + MLA_WIDTH]).astype(BF16)

    cq = _rms(p[:, COL_CQ:COL_CQ + MLA_Q_RANK]) * cqg_ref[...]
    qm = jnp.dot(cq.astype(BF16), w_uq_ref[...], preferred_element_type=F32)
    m_scale = MLA_QK ** -0.5
    cos_ms, sin_ms = cos_m * m_scale, sin_m * m_scale
    for i in range(MLA_HEADS):
        c0 = i * MLA_DK
        qm_ref[:, c0:c0 + HEAD_DIM] = (qm[:, c0:c0 + HEAD_DIM] * m_scale).astype(BF16)
        qm_ref[:, c0 + HEAD_DIM:c0 + MLA_DK] = _rot(
            qm[:, c0 + HEAD_DIM:c0 + MLA_DK], cos_ms, sin_ms).astype(BF16)

    ckv = _rms(p[:, COL_CKV:COL_CKV + MLA_KV_RANK]) * ckvg_ref[...]
    kvm = jnp.dot(ckv.astype(BF16), w_ukv_ref[...], preferred_element_type=F32)
    kr = _rot(p[:, COL_KR:COL_KR + HEAD_DIM], cos_m, sin_m).astype(BF16)
    for i in range(MLA_HEADS):
        c0 = i * MLA_DK
        km_ref[:, c0:c0 + HEAD_DIM] = kvm[:, i * MLA_NOPE:(i + 1) * MLA_NOPE].astype(BF16)
        km_ref[:, c0 + HEAD_DIM:c0 + MLA_DK] = kr
    vm_ref[...] = kvm[:, MLA_HEADS * MLA_NOPE:].astype(BF16)


def _mod_row(b, j):
    return jnp.where(j == N_BLK - 1, ADA_CTX_ROW, b)


def _in_proj(layer, xs, mod, norm_g, w_in_p, q_gain, k_gain, cq_gain, ckv_gain, w_uq_p, w_ukv_p, tables):
    n_b = xs.shape[0]
    tok = lambda width: pl.BlockSpec((None, TM, width), lambda b, j: (b, j, 0))
    lay = lambda *shape: pl.BlockSpec((None,) + shape, lambda b, j: (layer,) + (0,) * len(shape),
                                      pipeline_mode=pl.Buffered(1))
    tab = pl.BlockSpec((TM, HEAD_DIM), lambda b, j: (j, 0))
    out = lambda width: jax.ShapeDtypeStruct((n_b, T_TOK, width), BF16)
    return pl.pallas_call(
        _in_proj_kernel,
        grid=(n_b, N_BLK),
        in_specs=[
            tok(D_MODEL),
            pl.BlockSpec((None, None, 3, D_MODEL), lambda b, j: (layer, _mod_row(b, j), 0, 0)),
            lay(1, D_MODEL),
            lay(D_MODEL, D_IN_PACKED),
            lay(1, HEAD_DIM), lay(1, HEAD_DIM), lay(1, MLA_Q_RANK), lay(1, MLA_KV_RANK),
            lay(MLA_Q_RANK, MLA_QK_WIDTH),
            lay(MLA_KV_RANK, 2 * MLA_WIDTH),
            tab, tab, tab, tab,
        ],
        out_specs=[tok(GQA_WIDTH), tok(GQA_KV_WIDTH), tok(GQA_KV_WIDTH),
                   tok(MLA_QK_WIDTH), tok(MLA_QK_WIDTH), tok(MLA_WIDTH),
                   tok(GQA_WIDTH + MLA_WIDTH)],
        out_shape=[out(GQA_WIDTH), out(GQA_KV_WIDTH), out(GQA_KV_WIDTH),
                   out(MLA_QK_WIDTH), out(MLA_QK_WIDTH), out(MLA_WIDTH),
                   out(GQA_WIDTH + MLA_WIDTH)],
        compiler_params=pltpu.CompilerParams(
            dimension_semantics=("arbitrary", "arbitrary"), vmem_limit_bytes=VMEM_LIMIT),
        name="in_proj",
    )(xs, mod, norm_g, w_in_p, q_gain, k_gain, cq_gain, ckv_gain, w_uq_p, w_ukv_p, *tables)


def _attend(q_ref, k_ref, v_ref, g_ref, o_ref, *, key_rows, kv_heads, group, dk, dv):
    k0, k1 = key_rows
    for p in range(kv_heads):
        k = k_ref[k0:k1, p * dk:(p + 1) * dk]
        v = v_ref[k0:k1, p * dv:(p + 1) * dv]
        for i in range(group):
            hd = p * group + i
            q = q_ref[:, hd * dk:(hd + 1) * dk]
            s = lax.dot_general(q, k, (((1,), (1,)), ((), ())), preferred_element_type=F32)
            e = jnp.exp(s - jnp.max(s, axis=-1, keepdims=True))
            inv = 1.0 / jnp.sum(e, axis=-1, keepdims=True)
            o = jnp.dot(e.astype(BF16), v, preferred_element_type=F32) * inv
            gate = g_ref[:, hd * dv:(hd + 1) * dv].astype(F32)
            o_ref[:, hd * dv:(hd + 1) * dv] = (o * gate).astype(BF16)


def _attn_kernel(q_ref, k_ref, v_ref, g_ref, o_ref, *, blk_axis, **cfg):
    j = pl.program_id(blk_axis)

    @pl.when(j < N_LAT_BLK)
    def _():
        _attend(q_ref, k_ref, v_ref, g_ref, o_ref, key_rows=(0, T_TOK), **cfg)

    @pl.when(j == N_LAT_BLK)
    def _():
        _attend(q_ref, k_ref, v_ref, g_ref, o_ref, key_rows=(SEQ, T_TOK), **cfg)


def _attn_gqa(q, k, v, g, n_blk):
    n_b = q.shape[0]
    cfg = dict(kv_heads=GQA_KV_HEADS, group=GQA_GROUP, dk=HEAD_DIM, dv=HEAD_DIM)
    return pl.pallas_call(
        functools.partial(_attn_kernel, blk_axis=1, **cfg),
        grid=(n_b, n_blk),
        in_specs=[
            pl.BlockSpec((None, TM, GQA_WIDTH), lambda b, j: (b, j, 0)),
            pl.BlockSpec((None, T_TOK, GQA_KV_WIDTH), lambda b, j: (b, 0, 0)),
            pl.BlockSpec((None, T_TOK, GQA_KV_WIDTH), lambda b, j: (b, 0, 0)),
            pl.BlockSpec((None, TM, GQA_WIDTH), lambda b, j: (b, j, 0)),
        ],
        out_specs=pl.BlockSpec((None, TM, GQA_WIDTH), lambda b, j: (b, j, 0)),
        out_shape=jax.ShapeDtypeStruct((n_b, n_blk * TM, GQA_WIDTH), BF16),
        compiler_params=pltpu.CompilerParams(
            dimension_semantics=("arbitrary", "arbitrary"), vmem_limit_bytes=VMEM_LIMIT),
        name="attn_gqa",
    )(q, k, v, g)


def _attn_mla(q, k, v, g, n_blk):
    n_b = q.shape[0]
    hps = MLA_HEADS_PER_STEP
    n_hg = MLA_HEADS // hps
    g_col0 = GQA_WIDTH // (hps * MLA_V)
    cfg = dict(kv_heads=hps, group=1, dk=MLA_DK, dv=MLA_V)
    return pl.pallas_call(
        functools.partial(_attn_kernel, blk_axis=2, **cfg),
        grid=(n_b, n_hg, n_blk),
        in_specs=[
            pl.BlockSpec((None, TM, hps * MLA_DK), lambda b, hg, j: (b, j, hg)),
            pl.BlockSpec((None, T_TOK, hps * MLA_DK), lambda b, hg, j: (b, 0, hg)),
            pl.BlockSpec((None, T_TOK, hps * MLA_V), lambda b, hg, j: (b, 0, hg)),
            pl.BlockSpec((None, TM, hps * MLA_V), lambda b, hg, j: (b, j, g_col0 + hg)),
        ],
        out_specs=pl.BlockSpec((None, TM, hps * MLA_V), lambda b, hg, j: (b, j, hg)),
        out_shape=jax.ShapeDtypeStruct((n_b, n_blk * TM, MLA_WIDTH), BF16),
        compiler_params=pltpu.CompilerParams(
            dimension_semantics=("arbitrary", "arbitrary", "arbitrary"), vmem_limit_bytes=VMEM_LIMIT),
        name="attn_mla",
    )(q, k, v, g)


def _out_proj_kernel(ya_ref, ym_ref, w_ref, x_ref, mod_ref, o_ref):
    y = jnp.concatenate([ya_ref[...], ym_ref[...]], axis=-1)
    o_ref[...] = x_ref[...] + mod_ref[2:3, :] * jnp.dot(y, w_ref[...], preferred_element_type=F32)


def _out_proj(layer, ya, ym, w_out_b, xs, mod, n_blk):
    n_b = xs.shape[0]
    return pl.pallas_call(
        _out_proj_kernel,
        grid=(n_b, n_blk),
        in_specs=[
            pl.BlockSpec((None, TM, GQA_WIDTH), lambda b, j: (b, j, 0)),
            pl.BlockSpec((None, TM, MLA_WIDTH), lambda b, j: (b, j, 0)),
            pl.BlockSpec((None, GQA_WIDTH + MLA_WIDTH, D_MODEL), lambda b, j: (layer, 0, 0)),
            pl.BlockSpec((None, TM, D_MODEL), lambda b, j: (b, j, 0)),
            pl.BlockSpec((None, None, 3, D_MODEL), lambda b, j: (layer, _mod_row(b, j), 0, 0)),
        ],
        out_specs=pl.BlockSpec((None, TM, D_MODEL), lambda b, j: (b, j, 0)),
        out_shape=jax.ShapeDtypeStruct((n_b, n_blk * TM, D_MODEL), F32),
        compiler_params=pltpu.CompilerParams(
            dimension_semantics=("arbitrary", "arbitrary"), vmem_limit_bytes=VMEM_LIMIT),
        name="out_proj",
    )(ya, ym, w_out_b, xs, mod)


def _final_norm_kernel(x_ref, g_ref, o_ref):
    o_ref[...] = _rms(x_ref[...]) * g_ref[...]


def _final_norm(xs, final_g):
    n_b = xs.shape[0]
    return pl.pallas_call(
        _final_norm_kernel,
        grid=(n_b, N_LAT_BLK),
        in_specs=[pl.BlockSpec((None, TM, D_MODEL), lambda b, j: (b, j, 0)),
                  pl.BlockSpec((1, D_MODEL), lambda b, j: (0, 0))],
        out_specs=pl.BlockSpec((None, TM, D_MODEL), lambda b, j: (b, j, 0)),
        out_shape=jax.ShapeDtypeStruct((n_b, SEQ, D_MODEL), F32),
        compiler_params=pltpu.CompilerParams(dimension_semantics=("arbitrary", "arbitrary")),
        name="final_norm",
    )(xs, final_g.reshape(1, D_MODEL))


def _spread_rope(r):
    z = jnp.zeros(r.shape[:-1] + (MLA_ROPE // 2,), r.dtype)
    return jnp.concatenate([r[..., :MLA_ROPE // 2], z, r[..., MLA_ROPE // 2:], z], axis=-1)


def _pack_w_in(w_in):
    offs = [int(o) for o in np.cumsum(IN_SIZES)[:-1]]
    q, k, v, g_gqa, cq, ckv, kr, g_mla = jnp.split(w_in, offs, axis=-1)
    return jnp.concatenate([q, k, v, g_gqa, g_mla, cq, ckv, _spread_rope(kr)], axis=-1).astype(BF16)


def _pack_w_uq(w_uq):
    w = w_uq.reshape(DEPTH, MLA_Q_RANK, MLA_HEADS, MLA_QK)
    w = jnp.concatenate([w[..., :MLA_NOPE], _spread_rope(w[..., MLA_NOPE:])], axis=-1)
    return w.reshape(DEPTH, MLA_Q_RANK, MLA_QK_WIDTH).astype(BF16)


def _pack_w_ukv(w_ukv):
    w = w_ukv.reshape(DEPTH, MLA_KV_RANK, MLA_HEADS, MLA_NOPE + MLA_V)
    k_nope = w[..., :MLA_NOPE].reshape(DEPTH, MLA_KV_RANK, MLA_HEADS * MLA_NOPE)
    v = w[..., MLA_NOPE:].reshape(DEPTH, MLA_KV_RANK, MLA_WIDTH)
    return jnp.concatenate([k_nope, v], axis=-1).astype(BF16)


def _rope_tables():
    rows = SEQ // GRID_W
    row = jnp.repeat(jnp.arange(rows, dtype=F32), GRID_W)
    col = jnp.tile(jnp.arange(GRID_W, dtype=F32), rows)

    def cos_sin(rot_dim):
        n_freq = rot_dim // 4
        inv = ROPE_THETA ** (-jnp.arange(n_freq, dtype=F32) / n_freq)
        ang = jnp.concatenate([row[:, None] * inv[None], col[:, None] * inv[None]], axis=-1)
        return jnp.cos(ang), jnp.sin(ang)

    def with_ctx(cos_lat, sin_lat):
        return (jnp.concatenate([cos_lat, jnp.ones((CTX_LEN, HEAD_DIM), F32)], axis=0),
                jnp.concatenate([sin_lat, jnp.zeros((CTX_LEN, HEAD_DIM), F32)], axis=0))

    cos, sin = cos_sin(HEAD_DIM)
    cos_a, sin_a = with_ctx(jnp.concatenate([cos, cos], -1), jnp.concatenate([-sin, sin], -1))
    cos, sin = cos_sin(MLA_ROPE)
    cos_m, sin_m = with_ctx(_spread_rope(jnp.concatenate([cos, cos], -1)),
                            _spread_rope(jnp.concatenate([-sin, sin], -1)))
    return cos_a, sin_a, cos_m, sin_m


def kernel(x, c, ctx, c_ctx, w_ada, b_ada, norm_g, w_in, q_gain, k_gain, cq_gain, ckv_gain,
           w_uq, w_ukv, w_out, final_g):
    n_b = x.shape[0]
    assert x.shape == (n_b, SEQ, D_MODEL) and ctx.shape == (n_b, CTX_LEN, D_MODEL)
    assert n_b == ADA_CTX_ROW and w_ada.shape[0] == DEPTH

    c_all = jnp.concatenate(
        [c, c_ctx[None], jnp.zeros((ADA_ROWS - n_b - 1, D_MODEL), F32)], axis=0)
    mod = _ada_rows(c_all, w_ada, b_ada).reshape(DEPTH, ADA_ROWS, 3, D_MODEL)

    w_in_p = _pack_w_in(w_in)
    w_uq_p = _pack_w_uq(w_uq)
    w_ukv_p = _pack_w_ukv(w_ukv)
    w_out_b = w_out.astype(BF16)
    tables = _rope_tables()
    row3 = lambda a: a.reshape(DEPTH, 1, a.shape[-1])

    xs = jnp.concatenate([x, ctx], axis=1)
    for layer in range(DEPTH):
        n_blk = N_BLK if layer < DEPTH - 1 else N_LAT_BLK
        qa, ka, va, qm, km, vm, g = _in_proj(
            layer, xs, mod, row3(norm_g), w_in_p, row3(q_gain), row3(k_gain), row3(cq_gain),
            row3(ckv_gain), w_uq_p, w_ukv_p, tables)
        ya = _attn_gqa(qa, ka, va, g, n_blk)
        ym = _attn_mla(qm, km, vm, g, n_blk)
        xs_new = _out_proj(layer, ya, ym, w_out_b, xs, mod, n_blk)
        if layer < DEPTH - 1:
            xs = xs_new
        else:
            return _final_norm(xs_new, final_g)
```

```python
import functools

import jax
import jax.numpy as jnp
import numpy as np
from jax import lax
from jax.experimental import pallas as pl
from jax.experimental.pallas import tpu as pltpu

F32 = jnp.float32
BF16 = jnp.bfloat16

D_MODEL = 2048
SEQ = 2048
CTX_LEN = 256
T_TOK = SEQ + CTX_LEN
GRID_W = 64
ROPE_THETA = 10000.0
EPS = 1e-6
DEPTH = 4
LOG2_E = 1.4426950408889634

GQA_HEADS = 8
GQA_KV_HEADS = 2
GQA_GROUP = GQA_HEADS // GQA_KV_HEADS
HEAD_DIM = 128
GQA_WIDTH = GQA_HEADS * HEAD_DIM
GQA_KV_WIDTH = GQA_KV_HEADS * HEAD_DIM

MLA_HEADS = 8
MLA_NOPE = 128
MLA_ROPE = 64
MLA_QK = MLA_NOPE + MLA_ROPE
MLA_V = 128
MLA_Q_RANK = 512
MLA_KV_RANK = 512
MLA_WIDTH = MLA_HEADS * MLA_V
MLA_DK = 2 * HEAD_DIM
MLA_QK_WIDTH = MLA_HEADS * MLA_DK
V_EXT = 2 * HEAD_DIM

IN_SIZES = (GQA_WIDTH, GQA_KV_WIDTH, GQA_KV_WIDTH, GQA_WIDTH,
            MLA_Q_RANK, MLA_KV_RANK, MLA_ROPE, MLA_WIDTH)

COL_Q = 0
COL_K = COL_Q + GQA_WIDTH
COL_V = COL_K + GQA_KV_WIDTH
COL_G = COL_V + GQA_KV_WIDTH
COL_CQ = COL_G + GQA_WIDTH + MLA_WIDTH
COL_CKV = COL_CQ + MLA_Q_RANK
COL_KR = COL_CKV + MLA_KV_RANK
D_IN_PACKED = COL_KR + HEAD_DIM

ADA_ROWS = 8
ADA_CTX_ROW = 4
ADA_TN = 1024

TM = 256
TQ_GQA = 512
TQ_MLA = 512
N_BLK = T_TOK // TM
N_LAT_BLK = SEQ // TM
MLA_HEADS_PER_STEP = 4

VMEM_LIMIT = 56 * 1024 * 1024


def _silu(x):
    return x / (1.0 + jnp.exp(-x))


def _rms(x):
    return x * lax.rsqrt(jnp.mean(x * x, axis=-1, keepdims=True) + EPS)


def _rot(x, cos, sin):
    return x * cos + pltpu.roll(x, HEAD_DIM // 2, 1) * sin


def _ada_kernel(c_ref, w_ref, b_ref, o_ref):
    s = _silu(c_ref[...]).astype(BF16)
    w = w_ref[...].astype(BF16)
    o_ref[...] = jnp.dot(s, w, preferred_element_type=F32) + b_ref[...]


def _ada_rows(c_all, w_ada, b_ada):
    n_out = 3 * D_MODEL
    return pl.pallas_call(
        _ada_kernel,
        grid=(DEPTH, n_out // ADA_TN),
        in_specs=[
            pl.BlockSpec((ADA_ROWS, D_MODEL), lambda l, n: (0, 0)),
            pl.BlockSpec((None, D_MODEL, ADA_TN), lambda l, n: (l, 0, n)),
            pl.BlockSpec((None, 1, ADA_TN), lambda l, n: (l, 0, n)),
        ],
        out_specs=pl.BlockSpec((None, ADA_ROWS, ADA_TN), lambda l, n: (l, 0, n)),
        out_shape=jax.ShapeDtypeStruct((DEPTH, ADA_ROWS, n_out), F32),
        compiler_params=pltpu.CompilerParams(
            dimension_semantics=("arbitrary", "arbitrary"), vmem_limit_bytes=VMEM_LIMIT),
        name="ada_rows",
    )(c_all, w_ada, b_ada.reshape(DEPTH, 1, n_out))


def _in_proj_kernel(x_ref, mod_ref, ng_ref, w_in_ref, qg_ref, kg_ref, cqg_ref, ckvg_ref,
                    w_uq_ref, w_ukv_ref, cos_a_ref, sin_a_ref, cos_m_ref, sin_m_ref,
                    qa_ref, ka_ref, va_ref, qm_ref, km_ref, vm_ref, g_ref):
    x = x_ref[...]
    h = _rms(x) * ng_ref[...] * (1.0 + mod_ref[1:2, :]) + mod_ref[0:1, :]
    p = jnp.dot(h.astype(BF16), w_in_ref[...], preferred_element_type=F32)

    cos_a, sin_a = cos_a_ref[...], sin_a_ref[...]
    cos_m, sin_m = cos_m_ref[...], sin_m_ref[...]

    q_gain = qg_ref[...] * (HEAD_DIM ** -0.5 * LOG2_E)
    for i in range(GQA_HEADS):
        c0 = COL_Q + i * HEAD_DIM
        qh = _rms(p[:, c0:c0 + HEAD_DIM]) * q_gain
        qa_ref[:, i * HEAD_DIM:(i + 1) * HEAD_DIM] = _rot(qh, cos_a, sin_a).astype(BF16)
    k_gain = kg_ref[...]
    for i in range(GQA_KV_HEADS):
        c0 = COL_K + i * HEAD_DIM
        kh = _rms(p[:, c0:c0 + HEAD_DIM]) * k_gain
        ka_ref[:, i * HEAD_DIM:(i + 1) * HEAD_DIM] = _rot(kh, cos_a, sin_a).astype(BF16)
    ones = jnp.ones((TM, HEAD_DIM), BF16)
    for i in range(GQA_KV_HEADS):
        c0 = COL_V + i * HEAD_DIM
        va_ref[:, i * V_EXT:i * V_EXT + HEAD_DIM] = p[:, c0:c0 + HEAD_DIM].astype(BF16)
        va_ref[:, i * V_EXT + HEAD_DIM:(i + 1) * V_EXT] = ones

    g_ref[...] = _silu(p[:, COL_G:COL_G + GQA_WIDTH + MLA_WIDTH]).astype(BF16)

    cq = _rms(p[:, COL_CQ:COL_CQ + MLA_Q_RANK]) * cqg_ref[...]
    qm = jnp.dot(cq.astype(BF16), w_uq_ref[...], preferred_element_type=F32)
    m_scale = MLA_QK ** -0.5 * LOG2_E
    cos_ms, sin_ms = cos_m * m_scale, sin_m * m_scale
    for i in range(MLA_HEADS):
        c0 = i * MLA_DK
        qm_ref[:, c0:c0 + HEAD_DIM] = (qm[:, c0:c0 + HEAD_DIM] * m_scale).astype(BF16)
        qm_ref[:, c0 + HEAD_DIM:c0 + MLA_DK] = _rot(
            qm[:, c0 + HEAD_DIM:c0 + MLA_DK], cos_ms, sin_ms).astype(BF16)

    ckv = _rms(p[:, COL_CKV:COL_CKV + MLA_KV_RANK]) * ckvg_ref[...]
    kvm = jnp.dot(ckv.astype(BF16), w_ukv_ref[...], preferred_element_type=F32)
    kr = _rot(p[:, COL_KR:COL_KR + HEAD_DIM], cos_m, sin_m).astype(BF16)
    for i in range(MLA_HEADS):
        c0 = i * MLA_DK
        km_ref[:, c0:c0 + HEAD_DIM] = kvm[:, i * MLA_NOPE:(i + 1) * MLA_NOPE].astype(BF16)
        km_ref[:, c0 + HEAD_DIM:c0 + MLA_DK] = kr
        v0 = MLA_HEADS * MLA_NOPE + i * MLA_V
        vm_ref[:, i * V_EXT:i * V_EXT + MLA_V] = kvm[:, v0:v0 + MLA_V].astype(BF16)
        vm_ref[:, i * V_EXT + MLA_V:(i + 1) * V_EXT] = ones


def _mod_row(b, j):
    return jnp.where(j == N_BLK - 1, ADA_CTX_ROW, b)


def _in_proj(layer, xs, mod, norm_g, w_in_p, q_gain, k_gain, cq_gain, ckv_gain, w_uq_p, w_ukv_p, tables):
    n_b = xs.shape[0]
    tok = lambda width: pl.BlockSpec((None, TM, width), lambda b, j: (b, j, 0))
    lay = lambda *shape: pl.BlockSpec((None,) + shape, lambda b, j: (layer,) + (0,) * len(shape),
                                      pipeline_mode=pl.Buffered(1))
    tab = pl.BlockSpec((TM, HEAD_DIM), lambda b, j: (j, 0))
    out = lambda width: jax.ShapeDtypeStruct((n_b, T_TOK, width), BF16)
    return pl.pallas_call(
        _in_proj_kernel,
        grid=(n_b, N_BLK),
        in_specs=[
            tok(D_MODEL),
            pl.BlockSpec((None, None, 3, D_MODEL), lambda b, j: (layer, _mod_row(b, j), 0, 0)),
            lay(1, D_MODEL),
            lay(D_MODEL, D_IN_PACKED),
            lay(1, HEAD_DIM), lay(1, HEAD_DIM), lay(1, MLA_Q_RANK), lay(1, MLA_KV_RANK),
            lay(MLA_Q_RANK, MLA_QK_WIDTH),
            lay(MLA_KV_RANK, 2 * MLA_WIDTH),
            tab, tab, tab, tab,
        ],
        out_specs=[tok(GQA_WIDTH), tok(GQA_KV_WIDTH), tok(GQA_KV_HEADS * V_EXT),
                   tok(MLA_QK_WIDTH), tok(MLA_QK_WIDTH), tok(MLA_HEADS * V_EXT),
                   tok(GQA_WIDTH + MLA_WIDTH)],
        out_shape=[out(GQA_WIDTH), out(GQA_KV_WIDTH), out(GQA_KV_HEADS * V_EXT),
                   out(MLA_QK_WIDTH), out(MLA_QK_WIDTH), out(MLA_HEADS * V_EXT),
                   out(GQA_WIDTH + MLA_WIDTH)],
        compiler_params=pltpu.CompilerParams(
            dimension_semantics=("arbitrary", "arbitrary"), vmem_limit_bytes=VMEM_LIMIT),
        name="in_proj",
    )(xs, mod, norm_g, w_in_p, q_gain, k_gain, cq_gain, ckv_gain, w_uq_p, w_ukv_p, *tables)


def _attend(q_ref, k_ref, v_ref, g_ref, o_ref, *, kv_heads, group, dk, dv):
    for p in range(kv_heads):
        k = k_ref[:, p * dk:(p + 1) * dk]
        v = v_ref[:, p * V_EXT:(p + 1) * V_EXT]
        for r0 in range(0, q_ref.shape[0], TM):
            for i in range(group):
                hd = p * group + i
                q = q_ref[r0:r0 + TM, hd * dk:(hd + 1) * dk]
                s = lax.dot_general(q, k, (((1,), (1,)), ((), ())), preferred_element_type=F32)
                e = jnp.exp2(s - jnp.max(s, axis=-1, keepdims=True))
                o = jnp.dot(e.astype(BF16), v, preferred_element_type=F32)
                gate = g_ref[r0:r0 + TM, hd * dv:(hd + 1) * dv].astype(F32)
                o_ref[r0:r0 + TM, hd * dv:(hd + 1) * dv] = (
                    o[:, :dv] * (gate / o[:, dv:])).astype(BF16)


_GQA_CFG = dict(kv_heads=GQA_KV_HEADS, group=GQA_GROUP, dk=HEAD_DIM, dv=HEAD_DIM)


def _attn_gqa(q, k, v, g):
    n_b = q.shape[0]
    return pl.pallas_call(
        functools.partial(_attend, **_GQA_CFG),
        grid=(n_b, SEQ // TQ_GQA),
        in_specs=[
            pl.BlockSpec((None, TQ_GQA, GQA_WIDTH), lambda b, j: (b, j, 0)),
            pl.BlockSpec((None, T_TOK, GQA_KV_WIDTH), lambda b, j: (b, 0, 0)),
            pl.BlockSpec((None, T_TOK, GQA_KV_HEADS * V_EXT), lambda b, j: (b, 0, 0)),
            pl.BlockSpec((None, TQ_GQA, GQA_WIDTH), lambda b, j: (b, j, 0)),
        ],
        out_specs=pl.BlockSpec((None, TQ_GQA, GQA_WIDTH), lambda b, j: (b, j, 0)),
        out_shape=jax.ShapeDtypeStruct((n_b, SEQ, GQA_WIDTH), BF16),
        compiler_params=pltpu.CompilerParams(
            dimension_semantics=("arbitrary", "arbitrary"), vmem_limit_bytes=VMEM_LIMIT),
        name="attn_gqa",
    )(q, k, v, g)


def _attn_mla(q, k, v, g):
    n_b = q.shape[0]
    hps = MLA_HEADS_PER_STEP
    g_col0 = GQA_WIDTH // (hps * MLA_V)
    return pl.pallas_call(
        functools.partial(_attend, kv_heads=hps, group=1, dk=MLA_DK, dv=MLA_V),
        grid=(n_b, MLA_HEADS // hps, SEQ // TQ_MLA),
        in_specs=[
            pl.BlockSpec((None, TQ_MLA, hps * MLA_DK), lambda b, hg, j: (b, j, hg)),
            pl.BlockSpec((None, T_TOK, hps * MLA_DK), lambda b, hg, j: (b, 0, hg)),
            pl.BlockSpec((None, T_TOK, hps * V_EXT), lambda b, hg, j: (b, 0, hg)),
            pl.BlockSpec((None, TQ_MLA, hps * MLA_V), lambda b, hg, j: (b, j, g_col0 + hg)),
        ],
        out_specs=pl.BlockSpec((None, TQ_MLA, hps * MLA_V), lambda b, hg, j: (b, j, hg)),
        out_shape=jax.ShapeDtypeStruct((n_b, SEQ, MLA_WIDTH), BF16),
        compiler_params=pltpu.CompilerParams(
            dimension_semantics=("arbitrary", "arbitrary", "arbitrary"), vmem_limit_bytes=VMEM_LIMIT),
        name="attn_mla",
    )(q, k, v, g)


def _attn_ctx_kernel(qa_ref, ka_ref, va_ref, qm_ref, km_ref, vm_ref, g_ref, oa_ref, om_ref):
    _attend(qa_ref, ka_ref, va_ref, g_ref.at[:, :GQA_WIDTH], oa_ref, **_GQA_CFG)
    _attend(qm_ref, km_ref, vm_ref, g_ref.at[:, GQA_WIDTH:], om_ref,
            kv_heads=MLA_HEADS, group=1, dk=MLA_DK, dv=MLA_V)


def _attn_ctx(qa, ka, va, qm, km, vm, g):
    n_b = qa.shape[0]
    ctx_rows = lambda width: pl.BlockSpec((None, CTX_LEN, width), lambda b: (b, SEQ // CTX_LEN, 0))
    out = lambda width: pl.BlockSpec((None, CTX_LEN, width), lambda b: (b, 0, 0))
    return pl.pallas_call(
        _attn_ctx_kernel,
        grid=(n_b,),
        in_specs=[ctx_rows(GQA_WIDTH), ctx_rows(GQA_KV_WIDTH), ctx_rows(GQA_KV_HEADS * V_EXT),
                  ctx_rows(MLA_QK_WIDTH), ctx_rows(MLA_QK_WIDTH), ctx_rows(MLA_HEADS * V_EXT),
                  ctx_rows(GQA_WIDTH + MLA_WIDTH)],
        out_specs=[out(GQA_WIDTH), out(MLA_WIDTH)],
        out_shape=[jax.ShapeDtypeStruct((n_b, CTX_LEN, GQA_WIDTH), BF16),
                   jax.ShapeDtypeStruct((n_b, CTX_LEN, MLA_WIDTH), BF16)],
        compiler_params=pltpu.CompilerParams(
            dimension_semantics=("arbitrary",), vmem_limit_bytes=VMEM_LIMIT),
        name="attn_ctx",
    )(qa, ka, va, qm, km, vm, g)


def _out_proj_kernel(*refs, has_ctx):
    if has_ctx:
        ya_ref, ym_ref, yca_ref, ycm_ref, w_ref, x_ref, mod_ref, o_ref = refs
    else:
        ya_ref, ym_ref, w_ref, x_ref, mod_ref, o_ref = refs

    def project(a_ref, m_ref):
        y = jnp.concatenate([a_ref[...], m_ref[...]], axis=-1)
        o_ref[...] = x_ref[...] + mod_ref[2:3, :] * jnp.dot(y, w_ref[...], preferred_element_type=F32)

    if not has_ctx:
        project(ya_ref, ym_ref)
        return
    j = pl.program_id(1)
    pl.when(j < N_LAT_BLK)(lambda: project(ya_ref, ym_ref))
    pl.when(j == N_LAT_BLK)(lambda: project(yca_ref, ycm_ref))


def _out_proj(layer, y_lat, y_ctx, w_out_b, xs, mod):
    n_b = xs.shape[0]
    has_ctx = y_ctx is not None
    n_blk = N_BLK if has_ctx else N_LAT_BLK
    lat = lambda width: pl.BlockSpec((None, TM, width),
                                     lambda b, j: (b, jnp.minimum(j, N_LAT_BLK - 1), 0))
    ctx = lambda width: pl.BlockSpec((None, CTX_LEN, width), lambda b, j: (b, 0, 0))
    y_specs = [lat(GQA_WIDTH), lat(MLA_WIDTH)] + ([ctx(GQA_WIDTH), ctx(MLA_WIDTH)] if has_ctx else [])
    return pl.pallas_call(
        functools.partial(_out_proj_kernel, has_ctx=has_ctx),
        grid=(n_b, n_blk),
        in_specs=y_specs + [
            pl.BlockSpec((None, GQA_WIDTH + MLA_WIDTH, D_MODEL), lambda b, j: (layer, 0, 0)),
            pl.BlockSpec((None, TM, D_MODEL), lambda b, j: (b, j, 0)),
            pl.BlockSpec((None, None, 3, D_MODEL), lambda b, j: (layer, _mod_row(b, j), 0, 0)),
        ],
        out_specs=pl.BlockSpec((None, TM, D_MODEL), lambda b, j: (b, j, 0)),
        out_shape=jax.ShapeDtypeStruct((n_b, n_blk * TM, D_MODEL), F32),
        compiler_params=pltpu.CompilerParams(
            dimension_semantics=("arbitrary", "arbitrary"), vmem_limit_bytes=VMEM_LIMIT),
        name="out_proj",
    )(*y_lat, *(y_ctx or ()), w_out_b, xs, mod)


def _final_norm_kernel(x_ref, g_ref, o_ref):
    o_ref[...] = _rms(x_ref[...]) * g_ref[...]


def _final_norm(xs, final_g):
    n_b = xs.shape[0]
    return pl.pallas_call(
        _final_norm_kernel,
        grid=(n_b, N_LAT_BLK),
        in_specs=[pl.BlockSpec((None, TM, D_MODEL), lambda b, j: (b, j, 0)),
                  pl.BlockSpec((1, D_MODEL), lambda b, j: (0, 0))],
        out_specs=pl.BlockSpec((None, TM, D_MODEL), lambda b, j: (b, j, 0)),
        out_shape=jax.ShapeDtypeStruct((n_b, SEQ, D_MODEL), F32),
        compiler_params=pltpu.CompilerParams(dimension_semantics=("arbitrary", "arbitrary")),
        name="final_norm",
    )(xs, final_g.reshape(1, D_MODEL))


def _spread_rope(r):
    z = jnp.zeros(r.shape[:-1] + (MLA_ROPE // 2,), r.dtype)
    return jnp.concatenate([r[..., :MLA_ROPE // 2], z, r[..., MLA_ROPE // 2:], z], axis=-1)


def _pack_w_in(w_in):
    offs = [int(o) for o in np.cumsum(IN_SIZES)[:-1]]
    q, k, v, g_gqa, cq, ckv, kr, g_mla = jnp.split(w_in, offs, axis=-1)
    return jnp.concatenate([q, k, v, g_gqa, g_mla, cq, ckv, _spread_rope(kr)], axis=-1).astype(BF16)


def _pack_w_uq(w_uq):
    w = w_uq.reshape(DEPTH, MLA_Q_RANK, MLA_HEADS, MLA_QK)
    w = jnp.concatenate([w[..., :MLA_NOPE], _spread_rope(w[..., MLA_NOPE:])], axis=-1)
    return w.reshape(DEPTH, MLA_Q_RANK, MLA_QK_WIDTH).astype(BF16)


def _pack_w_ukv(w_ukv):
    w = w_ukv.reshape(DEPTH, MLA_KV_RANK, MLA_HEADS, MLA_NOPE + MLA_V)
    k_nope = w[..., :MLA_NOPE].reshape(DEPTH, MLA_KV_RANK, MLA_HEADS * MLA_NOPE)
    v = w[..., MLA_NOPE:].reshape(DEPTH, MLA_KV_RANK, MLA_WIDTH)
    return jnp.concatenate([k_nope, v], axis=-1).astype(BF16)


def _rope_tables():
    rows = SEQ // GRID_W
    row = jnp.repeat(jnp.arange(rows, dtype=F32), GRID_W)
    col = jnp.tile(jnp.arange(GRID_W, dtype=F32), rows)

    def cos_sin(rot_dim):
        n_freq = rot_dim // 4
        inv = ROPE_THETA ** (-jnp.arange(n_freq, dtype=F32) / n_freq)
        ang = jnp.concatenate([row[:, None] * inv[None], col[:, None] * inv[None]], axis=-1)
        return jnp.cos(ang), jnp.sin(ang)

    def with_ctx(cos_lat, sin_lat):
        return (jnp.concatenate([cos_lat, jnp.ones((CTX_LEN, HEAD_DIM), F32)], axis=0),
                jnp.concatenate([sin_lat, jnp.zeros((CTX_LEN, HEAD_DIM), F32)], axis=0))

    cos, sin = cos_sin(HEAD_DIM)
    cos_a, sin_a = with_ctx(jnp.concatenate([cos, cos], -1), jnp.concatenate([-sin, sin], -1))
    cos, sin = cos_sin(MLA_ROPE)
    cos_m, sin_m = with_ctx(_spread_rope(jnp.concatenate([cos, cos], -1)),
                            _spread_rope(jnp.concatenate([-sin, sin], -1)))
    return cos_a, sin_a, cos_m, sin_m


def kernel(x, c, ctx, c_ctx, w_ada, b_ada, norm_g, w_in, q_gain, k_gain, cq_gain, ckv_gain,
           w_uq, w_ukv, w_out, final_g):
    n_b = x.shape[0]
    assert x.shape == (n_b, SEQ, D_MODEL) and ctx.shape == (n_b, CTX_LEN, D_MODEL)
    assert n_b == ADA_CTX_ROW and w_ada.shape[0] == DEPTH

    c_all = jnp.concatenate(
        [c, c_ctx[None], jnp.zeros((ADA_ROWS - n_b - 1, D_MODEL), F32)], axis=0)
    mod = _ada_rows(c_all, w_ada, b_ada).reshape(DEPTH, ADA_ROWS, 3, D_MODEL)

    w_in_p = _pack_w_in(w_in)
    w_uq_p = _pack_w_uq(w_uq)
    w_ukv_p = _pack_w_ukv(w_ukv)
    w_out_b = w_out.astype(BF16)
    tables = _rope_tables()
    row3 = lambda a: a.reshape(DEPTH, 1, a.shape[-1])

    xs = jnp.concatenate([x, ctx], axis=1)
    for layer in range(DEPTH):
        last = layer == DEPTH - 1
        qa, ka, va, qm, km, vm, g = _in_proj(
            layer, xs, mod, row3(norm_g), w_in_p, row3(q_gain), row3(k_gain), row3(cq_gain),
            row3(ckv_gain), w_uq_p, w_ukv_p, tables)
        y_lat = (_attn_gqa(qa, ka, va, g), _attn_mla(qm, km, vm, g))
        y_ctx = None if last else _attn_ctx(qa, ka, va, qm, km, vm, g)
        xs = _out_proj(layer, y_lat, y_ctx, w_out_b, xs, mod)
    return _final_norm(xs, final_g)
```

```python
import functools

import jax
import jax.numpy as jnp
from jax import lax
from jax.experimental import pallas as pl
from jax.experimental.pallas import tpu as pltpu

F32 = jnp.float32
BF16 = jnp.bfloat16

D_MODEL = 2048
SEQ = 2048
CTX_LEN = 256
T_TOK = SEQ + CTX_LEN
GRID_W = 64
ROPE_THETA = 10000.0
EPS = 1e-6
DEPTH = 4
LOG2_E = 1.4426950408889634

LANES = 128
HEAD_DIM = 128
GQA_HEADS = 8
GQA_KV_HEADS = 2
GQA_GROUP = GQA_HEADS // GQA_KV_HEADS
GQA_WIDTH = GQA_HEADS * HEAD_DIM
GQA_KV_WIDTH = GQA_KV_HEADS * HEAD_DIM

MLA_HEADS = 8
MLA_NOPE = 128
MLA_ROPE = 64
MLA_QK = MLA_NOPE + MLA_ROPE
MLA_V = 128
MLA_Q_RANK = 512
MLA_KV_RANK = 512
MLA_WIDTH = MLA_HEADS * MLA_V
D_MIX = GQA_WIDTH + MLA_WIDTH
MLA_DK = MLA_NOPE + LANES
MLA_QK_WIDTH = MLA_HEADS * MLA_DK
V_EXT = 2 * HEAD_DIM

COL_Q = 0
COL_K = COL_Q + GQA_WIDTH
COL_V = COL_K + GQA_KV_WIDTH
COL_GG = COL_V + GQA_KV_WIDTH
COL_CQ = COL_GG + GQA_WIDTH
COL_CKV = COL_CQ + MLA_Q_RANK
COL_KR = COL_CKV + MLA_KV_RANK
COL_GM = COL_KR + LANES
D_IN_PACKED = COL_GM + MLA_WIDTH
D_IN = D_IN_PACKED - (LANES - MLA_ROPE)

TAB_COS_A, TAB_SIN_A, TAB_COS_M, TAB_SIN_M_LO, TAB_SIN_M_HI = range(5)
TAB_WIDTH = 5 * LANES

ADA_ROWS = 8
ADA_CTX_ROW = 4
ADA_TN = 1024

TM = 256
TQ_GQA = 512
TQ_MLA = 512
N_BLK = T_TOK // TM
N_LAT_BLK = SEQ // TM
MLA_HEADS_PER_STEP = 4

VMEM_LIMIT = 56 * 1024 * 1024

assert CTX_LEN == TM and SEQ % TQ_GQA == 0 and SEQ % TQ_MLA == 0


def _silu(x):
    return x / (1.0 + jnp.exp(-x))


def _rms(x):
    return x * lax.rsqrt(jnp.mean(x * x, axis=-1, keepdims=True) + EPS)


def _tab(tab_ref, col):
    return tab_ref[:, col * LANES:(col + 1) * LANES]


def _rot_a(x, tab_ref):
    return x * _tab(tab_ref, TAB_COS_A) + pltpu.roll(x, HEAD_DIM // 2, 1) * _tab(tab_ref, TAB_SIN_A)


def _rot_m(x, tab_ref, scale=None):
    half = MLA_ROPE // 2
    y = (x * _tab(tab_ref, TAB_COS_M)
         + pltpu.roll(x, LANES - half, 1) * _tab(tab_ref, TAB_SIN_M_LO)
         + pltpu.roll(x, half, 1) * _tab(tab_ref, TAB_SIN_M_HI))
    return y if scale is None else y * scale


def _mod_row(b, j):
    return jnp.where(j == N_LAT_BLK, ADA_CTX_ROW, b)


def _stream_specs(ctx_blk):
    return [pl.BlockSpec((None, TM, D_MODEL), lambda b, j: (b, jnp.minimum(j, N_LAT_BLK - 1), 0)),
            pl.BlockSpec((None, CTX_LEN, D_MODEL), lambda b, j: (b, ctx_blk, 0))]


def _stream_block(xl_ref, xc_ref):
    return jnp.where(pl.program_id(1) == N_LAT_BLK, xc_ref[...], xl_ref[...])


def _ada_kernel(c_ref, w_ref, b_ref, o_ref):
    s = _silu(c_ref[...]).astype(BF16)
    w = w_ref[...].astype(BF16)
    o_ref[...] = jnp.dot(s, w, preferred_element_type=F32) + b_ref[...]


def _ada_rows(c_all, w_ada, b_ada):
    n_out = 3 * D_MODEL
    return pl.pallas_call(
        _ada_kernel,
        grid=(DEPTH, n_out // ADA_TN),
        in_specs=[
            pl.BlockSpec((ADA_ROWS, D_MODEL), lambda l, n: (0, 0)),
            pl.BlockSpec((None, D_MODEL, ADA_TN), lambda l, n: (l, 0, n)),
            pl.BlockSpec((None, 1, ADA_TN), lambda l, n: (l, 0, n)),
        ],
        out_specs=pl.BlockSpec((None, ADA_ROWS, ADA_TN), lambda l, n: (l, 0, n)),
        out_shape=jax.ShapeDtypeStruct((DEPTH, ADA_ROWS, n_out), F32),
        compiler_params=pltpu.CompilerParams(
            dimension_semantics=("arbitrary", "arbitrary"), vmem_limit_bytes=VMEM_LIMIT),
        name="ada_rows",
    )(c_all, w_ada, b_ada.reshape(DEPTH, 1, n_out))


def _in_proj_kernel(xl_ref, xc_ref, mod_ref, ng_ref, w_in_ref, qg_ref, kg_ref, cqg_ref, ckvg_ref,
                    w_uq_ref, w_ukv_ref, tab_ref,
                    qa_ref, ka_ref, va_ref, qm_ref, km_ref, vm_ref, g_ref):
    x = _stream_block(xl_ref, xc_ref)
    h = _rms(x) * ng_ref[...] * (1.0 + mod_ref[1:2, :]) + mod_ref[0:1, :]
    p = jnp.dot(h.astype(BF16), w_in_ref[...], preferred_element_type=F32)

    q_gain = qg_ref[...] * (HEAD_DIM ** -0.5 * LOG2_E)
    for i in range(GQA_HEADS):
        c0 = COL_Q + i * HEAD_DIM
        qh = _rms(p[:, c0:c0 + HEAD_DIM]) * q_gain
        qa_ref[:, i * HEAD_DIM:(i + 1) * HEAD_DIM] = _rot_a(qh, tab_ref).astype(BF16)
    k_gain = kg_ref[...]
    for i in range(GQA_KV_HEADS):
        c0 = COL_K + i * HEAD_DIM
        kh = _rms(p[:, c0:c0 + HEAD_DIM]) * k_gain
        ka_ref[:, i * HEAD_DIM:(i + 1) * HEAD_DIM] = _rot_a(kh, tab_ref).astype(BF16)
    ones = jnp.ones((TM, HEAD_DIM), BF16)
    for i in range(GQA_KV_HEADS):
        c0 = COL_V + i * HEAD_DIM
        va_ref[:, i * V_EXT:i * V_EXT + HEAD_DIM] = p[:, c0:c0 + HEAD_DIM].astype(BF16)
        va_ref[:, i * V_EXT + HEAD_DIM:(i + 1) * V_EXT] = ones

    g_ref[:, :GQA_WIDTH] = _silu(p[:, COL_GG:COL_GG + GQA_WIDTH]).astype(BF16)
    g_ref[:, GQA_WIDTH:] = _silu(p[:, COL_GM:COL_GM + MLA_WIDTH]).astype(BF16)

    cq = _rms(p[:, COL_CQ:COL_CQ + MLA_Q_RANK]) * cqg_ref[...]
    qm = jnp.dot(cq.astype(BF16), w_uq_ref[...], preferred_element_type=F32)
    m_scale = MLA_QK ** -0.5 * LOG2_E
    for i in range(MLA_HEADS):
        c0 = i * MLA_DK
        qm_ref[:, c0:c0 + MLA_NOPE] = (qm[:, c0:c0 + MLA_NOPE] * m_scale).astype(BF16)
        qm_ref[:, c0 + MLA_NOPE:c0 + MLA_DK] = _rot_m(
            qm[:, c0 + MLA_NOPE:c0 + MLA_DK], tab_ref, m_scale).astype(BF16)

    ckv = _rms(p[:, COL_CKV:COL_CKV + MLA_KV_RANK]) * ckvg_ref[...]
    kvm = jnp.dot(ckv.astype(BF16), w_ukv_ref[...], preferred_element_type=F32)
    kr = _rot_m(p[:, COL_KR:COL_KR + LANES], tab_ref).astype(BF16)
    for i in range(MLA_HEADS):
        c0 = i * (MLA_NOPE + MLA_V)
        km_ref[:, i * MLA_DK:i * MLA_DK + MLA_NOPE] = kvm[:, c0:c0 + MLA_NOPE].astype(BF16)
        km_ref[:, i * MLA_DK + MLA_NOPE:(i + 1) * MLA_DK] = kr
        vm_ref[:, i * V_EXT:i * V_EXT + MLA_V] = kvm[:, c0 + MLA_NOPE:c0 + MLA_NOPE + MLA_V].astype(BF16)
        vm_ref[:, i * V_EXT + MLA_V:(i + 1) * V_EXT] = ones


def _in_proj(layer, stream, ctx_blk, mod, params, tab):
    n_b = stream[0].shape[0]
    tok = lambda width: pl.BlockSpec((None, TM, width), lambda b, j: (b, j, 0))
    lay = lambda *shape: pl.BlockSpec((None,) + shape, lambda b, j: (layer,) + (0,) * len(shape),
                                      pipeline_mode=pl.Buffered(1))
    widths = (GQA_WIDTH, GQA_KV_WIDTH, GQA_KV_HEADS * V_EXT,
              MLA_QK_WIDTH, MLA_QK_WIDTH, MLA_HEADS * V_EXT, D_MIX)
    return pl.pallas_call(
        _in_proj_kernel,
        grid=(n_b, N_BLK),
        in_specs=_stream_specs(ctx_blk) + [
            pl.BlockSpec((None, None, 3, D_MODEL), lambda b, j: (layer, _mod_row(b, j), 0, 0)),
            lay(1, D_MODEL),
            lay(D_MODEL, D_IN_PACKED),
            lay(1, HEAD_DIM), lay(1, HEAD_DIM), lay(1, MLA_Q_RANK), lay(1, MLA_KV_RANK),
            lay(MLA_Q_RANK, MLA_QK_WIDTH),
            lay(MLA_KV_RANK, MLA_HEADS * (MLA_NOPE + MLA_V)),
            pl.BlockSpec((TM, TAB_WIDTH), lambda b, j: (j, 0)),
        ],
        out_specs=[tok(w) for w in widths],
        out_shape=[jax.ShapeDtypeStruct((n_b, T_TOK, w), BF16) for w in widths],
        compiler_params=pltpu.CompilerParams(
            dimension_semantics=("arbitrary", "arbitrary"), vmem_limit_bytes=VMEM_LIMIT),
        name="in_proj",
    )(*stream, mod, *params, tab)


def _attend(q_ref, k_ref, v_ref, g_ref, o_ref, *, kv_heads, group, dk, dv):
    for p in range(kv_heads):
        k = k_ref[:, p * dk:(p + 1) * dk]
        v = v_ref[:, p * V_EXT:(p + 1) * V_EXT]
        for r0 in range(0, q_ref.shape[0], TM):
            for i in range(group):
                hd = p * group + i
                q = q_ref[r0:r0 + TM, hd * dk:(hd + 1) * dk]
                s = lax.dot_general(q, k, (((1,), (1,)), ((), ())), preferred_element_type=F32)
                e = jnp.exp2(s - jnp.max(s, axis=-1, keepdims=True))
                o = jnp.dot(e.astype(BF16), v, preferred_element_type=F32)
                gate = g_ref[r0:r0 + TM, hd * dv:(hd + 1) * dv].astype(F32)
                o_ref[r0:r0 + TM, hd * dv:(hd + 1) * dv] = (
                    o[:, :dv] * (gate / o[:, dv:])).astype(BF16)


_GQA_CFG = dict(kv_heads=GQA_KV_HEADS, group=GQA_GROUP, dk=HEAD_DIM, dv=HEAD_DIM)


def _attn_gqa(q, k, v, g):
    n_b = q.shape[0]
    return pl.pallas_call(
        functools.partial(_attend, **_GQA_CFG),
        grid=(n_b, SEQ // TQ_GQA),
        in_specs=[
            pl.BlockSpec((None, TQ_GQA, GQA_WIDTH), lambda b, j: (b, j, 0)),
            pl.BlockSpec((None, T_TOK, GQA_KV_WIDTH), lambda b, j: (b, 0, 0)),
            pl.BlockSpec((None, T_TOK, GQA_KV_HEADS * V_EXT), lambda b, j: (b, 0, 0)),
            pl.BlockSpec((None, TQ_GQA, GQA_WIDTH), lambda b, j: (b, j, 0)),
        ],
        out_specs=pl.BlockSpec((None, TQ_GQA, GQA_WIDTH), lambda b, j: (b, j, 0)),
        out_shape=jax.ShapeDtypeStruct((n_b, SEQ, GQA_WIDTH), BF16),
        compiler_params=pltpu.CompilerParams(
            dimension_semantics=("arbitrary", "arbitrary"), vmem_limit_bytes=VMEM_LIMIT),
        name="attn_gqa",
    )(q, k, v, g)


def _attn_mla(q, k, v, g):
    n_b = q.shape[0]
    hps = MLA_HEADS_PER_STEP
    g_col0 = GQA_WIDTH // (hps * MLA_V)
    return pl.pallas_call(
        functools.partial(_attend, kv_heads=hps, group=1, dk=MLA_DK, dv=MLA_V),
        grid=(n_b, MLA_HEADS // hps, SEQ // TQ_MLA),
        in_specs=[
            pl.BlockSpec((None, TQ_MLA, hps * MLA_DK), lambda b, hg, j: (b, j, hg)),
            pl.BlockSpec((None, T_TOK, hps * MLA_DK), lambda b, hg, j: (b, 0, hg)),
            pl.BlockSpec((None, T_TOK, hps * V_EXT), lambda b, hg, j: (b, 0, hg)),
            pl.BlockSpec((None, TQ_MLA, hps * MLA_V), lambda b, hg, j: (b, j, g_col0 + hg)),
        ],
        out_specs=pl.BlockSpec((None, TQ_MLA, hps * MLA_V), lambda b, hg, j: (b, j, hg)),
        out_shape=jax.ShapeDtypeStruct((n_b, SEQ, MLA_WIDTH), BF16),
        compiler_params=pltpu.CompilerParams(
            dimension_semantics=("arbitrary", "arbitrary", "arbitrary"), vmem_limit_bytes=VMEM_LIMIT),
        name="attn_mla",
    )(q, k, v, g)


def _attn_ctx_kernel(qa_ref, ka_ref, va_ref, qm_ref, km_ref, vm_ref, g_ref, oa_ref, om_ref):
    _attend(qa_ref, ka_ref, va_ref, g_ref.at[:, :GQA_WIDTH], oa_ref, **_GQA_CFG)
    _attend(qm_ref, km_ref, vm_ref, g_ref.at[:, GQA_WIDTH:], om_ref,
            kv_heads=MLA_HEADS, group=1, dk=MLA_DK, dv=MLA_V)


def _attn_ctx(qa, ka, va, qm, km, vm, g):
    n_b = qa.shape[0]
    ctx_rows = lambda width: pl.BlockSpec((None, CTX_LEN, width), lambda b: (b, SEQ // CTX_LEN, 0))
    out = lambda width: pl.BlockSpec((None, CTX_LEN, width), lambda b: (b, 0, 0))
    return pl.pallas_call(
        _attn_ctx_kernel,
        grid=(n_b,),
        in_specs=[ctx_rows(GQA_WIDTH), ctx_rows(GQA_KV_WIDTH), ctx_rows(GQA_KV_HEADS * V_EXT),
                  ctx_rows(MLA_QK_WIDTH), ctx_rows(MLA_QK_WIDTH), ctx_rows(MLA_HEADS * V_EXT),
                  ctx_rows(D_MIX)],
        out_specs=[out(GQA_WIDTH), out(MLA_WIDTH)],
        out_shape=[jax.ShapeDtypeStruct((n_b, CTX_LEN, GQA_WIDTH), BF16),
                   jax.ShapeDtypeStruct((n_b, CTX_LEN, MLA_WIDTH), BF16)],
        compiler_params=pltpu.CompilerParams(
            dimension_semantics=("arbitrary",), vmem_limit_bytes=VMEM_LIMIT),
        name="attn_ctx",
    )(qa, ka, va, qm, km, vm, g)


def _out_proj_kernel(*refs, last):
    if last:
        ya_ref, ym_ref, w_ref, x_ref, mod_ref, fg_ref, o_ref = refs
        a, m, x = ya_ref[...], ym_ref[...], x_ref[...]
    else:
        ya_ref, ym_ref, yca_ref, ycm_ref, w_ref, xl_ref, xc_ref, mod_ref, o_ref = refs
        is_ctx = pl.program_id(1) == N_LAT_BLK
        a = jnp.where(is_ctx, yca_ref[...], ya_ref[...])
        m = jnp.where(is_ctx, ycm_ref[...], ym_ref[...])
        x = _stream_block(xl_ref, xc_ref)
    y = jnp.concatenate([a, m], axis=-1)
    x = x + mod_ref[2:3, :] * jnp.dot(y, w_ref[...], preferred_element_type=F32)
    o_ref[...] = _rms(x) * fg_ref[...] if last else x


def _out_proj(layer, y_lat, y_ctx, w_out_b, stream, ctx_blk, mod, final_g):
    n_b = y_lat[0].shape[0]
    last = y_ctx is None
    lat = lambda width: pl.BlockSpec((None, TM, width),
                                     lambda b, j: (b, jnp.minimum(j, N_LAT_BLK - 1), 0))
    ctx = lambda width: pl.BlockSpec((None, CTX_LEN, width), lambda b, j: (b, 0, 0))
    w_spec = pl.BlockSpec((None, D_MIX, D_MODEL), lambda b, j: (layer, 0, 0))
    mod_spec = pl.BlockSpec((None, None, 3, D_MODEL), lambda b, j: (layer, _mod_row(b, j), 0, 0))
    if last:
        n_blk = N_LAT_BLK
        in_specs = [lat(GQA_WIDTH), lat(MLA_WIDTH), w_spec, _stream_specs(ctx_blk)[0], mod_spec,
                    pl.BlockSpec((1, D_MODEL), lambda b, j: (0, 0))]
        args = (*y_lat, w_out_b, stream[0], mod, final_g.reshape(1, D_MODEL))
    else:
        n_blk = N_BLK
        in_specs = [lat(GQA_WIDTH), lat(MLA_WIDTH), ctx(GQA_WIDTH), ctx(MLA_WIDTH), w_spec,
                    *_stream_specs(ctx_blk), mod_spec]
        args = (*y_lat, *y_ctx, w_out_b, *stream, mod)
    return pl.pallas_call(
        functools.partial(_out_proj_kernel, last=last),
        grid=(n_b, n_blk),
        in_specs=in_specs,
        out_specs=pl.BlockSpec((None, TM, D_MODEL), lambda b, j: (b, j, 0)),
        out_shape=jax.ShapeDtypeStruct((n_b, n_blk * TM, D_MODEL), F32),
        compiler_params=pltpu.CompilerParams(
            dimension_semantics=("arbitrary", "arbitrary"), vmem_limit_bytes=VMEM_LIMIT),
        name="out_proj",
    )(*args)


def _pack_w_in(w_in):
    cut = COL_KR + MLA_ROPE
    pad = jnp.zeros(w_in.shape[:-1] + (LANES - MLA_ROPE,), w_in.dtype)
    return jnp.concatenate([w_in[..., :cut], pad, w_in[..., cut:]], axis=-1).astype(BF16)


def _pack_w_uq(w_uq):
    w = w_uq.reshape(DEPTH, MLA_Q_RANK, MLA_HEADS, MLA_QK)
    w = jnp.pad(w, ((0, 0), (0, 0), (0, 0), (0, MLA_DK - MLA_QK)))
    return w.reshape(DEPTH, MLA_Q_RANK, MLA_QK_WIDTH).astype(BF16)


def _rope_table():
    rows = SEQ // GRID_W
    row = jnp.repeat(jnp.arange(rows, dtype=F32), GRID_W)
    col = jnp.tile(jnp.arange(GRID_W, dtype=F32), rows)

    def cos_sin(rot_dim):
        n_freq = rot_dim // 4
        inv = ROPE_THETA ** (-jnp.arange(n_freq, dtype=F32) / n_freq)
        ang = jnp.concatenate([row[:, None] * inv[None], col[:, None] * inv[None]], axis=-1)
        return jnp.cos(ang), jnp.sin(ang)

    cos, sin = cos_sin(HEAD_DIM)
    cos_m, sin_m = cos_sin(MLA_ROPE)
    z32 = jnp.zeros_like(sin_m)
    z64 = jnp.zeros_like(sin)
    lat = jnp.concatenate([
        cos, cos,
        -sin, sin,
        cos_m, cos_m, z64,
        -sin_m, z32, z64,
        z32, sin_m, z64,
    ], axis=-1)
    ctx_row = jnp.concatenate([jnp.ones((LANES,), F32), jnp.zeros((LANES,), F32),
                               jnp.ones((LANES,), F32), jnp.zeros((2 * LANES,), F32)])
    return jnp.concatenate([lat, jnp.broadcast_to(ctx_row, (CTX_LEN, TAB_WIDTH))], axis=0)


def kernel(x, c, ctx, c_ctx, w_ada, b_ada, norm_g, w_in, q_gain, k_gain, cq_gain, ckv_gain,
           w_uq, w_ukv, w_out, final_g):
    n_b = x.shape[0]
    assert x.shape == (n_b, SEQ, D_MODEL) and ctx.shape == (n_b, CTX_LEN, D_MODEL)
    assert n_b == ADA_CTX_ROW and w_ada.shape[0] == DEPTH and w_in.shape[-1] == D_IN

    c_all = jnp.concatenate(
        [c, c_ctx[None], jnp.zeros((ADA_ROWS - n_b - 1, D_MODEL), F32)], axis=0)
    mod = _ada_rows(c_all, w_ada, b_ada).reshape(DEPTH, ADA_ROWS, 3, D_MODEL)

    row3 = lambda a: a.reshape(DEPTH, 1, a.shape[-1])
    params = (row3(norm_g), _pack_w_in(w_in), row3(q_gain), row3(k_gain), row3(cq_gain),
              row3(ckv_gain), _pack_w_uq(w_uq), w_ukv.astype(BF16))
    w_out_b = w_out.astype(BF16)
    tab = _rope_table()

    stream, ctx_blk = (x, ctx), 0
    for layer in range(DEPTH):
        last = layer == DEPTH - 1
        qa, ka, va, qm, km, vm, g = _in_proj(layer, stream, ctx_blk, mod, params, tab)
        y_lat = (_attn_gqa(qa, ka, va, g), _attn_mla(qm, km, vm, g))
        y_ctx = None if last else _attn_ctx(qa, ka, va, qm, km, vm, g)
        xs = _out_proj(layer, y_lat, y_ctx, w_out_b, stream, ctx_blk, mod, final_g)
        stream, ctx_blk = (xs, xs), SEQ // CTX_LEN
    return xs
```

```python
import functools

import jax
import jax.numpy as jnp
from jax import lax
from jax.experimental import pallas as pl
from jax.experimental.pallas import tpu as pltpu

F32 = jnp.float32
BF16 = jnp.bfloat16

D_MODEL = 2048
SEQ = 2048
CTX_LEN = 256
T_TOK = SEQ + CTX_LEN
GRID_W = 64
ROPE_THETA = 10000.0
EPS = 1e-6
DEPTH = 4
LOG2_E = 1.4426950408889634

LANES = 128
HEAD_DIM = 128
GQA_HEADS = 8
GQA_KV_HEADS = 2
GQA_GROUP = GQA_HEADS // GQA_KV_HEADS
GQA_WIDTH = GQA_HEADS * HEAD_DIM
GQA_KV_WIDTH = GQA_KV_HEADS * HEAD_DIM

MLA_HEADS = 8
MLA_NOPE = 128
MLA_ROPE = 64
MLA_QK = MLA_NOPE + MLA_ROPE
MLA_V = 128
MLA_Q_RANK = 512
MLA_KV_RANK = 512
MLA_WIDTH = MLA_HEADS * MLA_V
D_MIX = GQA_WIDTH + MLA_WIDTH
MLA_DK = MLA_NOPE + LANES
MLA_QK_WIDTH = MLA_HEADS * MLA_DK
V_EXT = 2 * HEAD_DIM

COL_Q = 0
COL_K = COL_Q + GQA_WIDTH
COL_V = COL_K + GQA_KV_WIDTH
COL_GG = COL_V + GQA_KV_WIDTH
COL_CQ = COL_GG + GQA_WIDTH
COL_CKV = COL_CQ + MLA_Q_RANK
COL_KR = COL_CKV + MLA_KV_RANK
COL_GM = COL_KR + LANES
D_IN_PACKED = COL_GM + MLA_WIDTH
D_IN = D_IN_PACKED - (LANES - MLA_ROPE)

TAB_COS_A, TAB_SIN_A, TAB_COS_M, TAB_SIN_M_LO, TAB_SIN_M_HI = range(5)
TAB_WIDTH = 5 * LANES

ADA_ROWS = 8
ADA_CTX_ROW = 4
ADA_TN = 1024
PACK_ROWS = 512

TM = 256
TQ_GQA = 512
TQ_MLA = 1024
N_BLK = T_TOK // TM
N_LAT_BLK = SEQ // TM
MLA_HEADS_PER_STEP = 4

VMEM_LIMIT = 56 * 1024 * 1024

assert CTX_LEN == TM and SEQ % TQ_GQA == 0 and SEQ % TQ_MLA == 0


def _silu(x):
    return x / (1.0 + jnp.exp(-x))


def _rms(x):
    return x * lax.rsqrt(jnp.mean(x * x, axis=-1, keepdims=True) + EPS)


def _tab(tab_ref, col):
    return tab_ref[:, col * LANES:(col + 1) * LANES]


def _rot_a(x, tab_ref):
    return x * _tab(tab_ref, TAB_COS_A) + pltpu.roll(x, HEAD_DIM // 2, 1) * _tab(tab_ref, TAB_SIN_A)


def _rot_m(x, tab_ref, scale=None):
    half = MLA_ROPE // 2
    y = (x * _tab(tab_ref, TAB_COS_M)
         + pltpu.roll(x, LANES - half, 1) * _tab(tab_ref, TAB_SIN_M_LO)
         + pltpu.roll(x, half, 1) * _tab(tab_ref, TAB_SIN_M_HI))
    return y if scale is None else y * scale


def _mod_row(b, j):
    return jnp.where(j == N_LAT_BLK, ADA_CTX_ROW, b)


def _stream_specs(ctx_blk):
    return [pl.BlockSpec((None, TM, D_MODEL), lambda b, j: (b, jnp.minimum(j, N_LAT_BLK - 1), 0)),
            pl.BlockSpec((None, CTX_LEN, D_MODEL), lambda b, j: (b, ctx_blk, 0))]


def _stream_block(xl_ref, xc_ref):
    return jnp.where(pl.program_id(1) == N_LAT_BLK, xc_ref[...], xl_ref[...])


def _ada_kernel(c_ref, w_ref, b_ref, o_ref):
    s = _silu(c_ref[...]).astype(BF16)
    w = w_ref[...].astype(BF16)
    o_ref[...] = jnp.dot(s, w, preferred_element_type=F32) + b_ref[...]


def _ada_rows(c_all, w_ada, b_ada):
    n_out = 3 * D_MODEL
    return pl.pallas_call(
        _ada_kernel,
        grid=(DEPTH, n_out // ADA_TN),
        in_specs=[
            pl.BlockSpec((ADA_ROWS, D_MODEL), lambda l, n: (0, 0)),
            pl.BlockSpec((None, D_MODEL, ADA_TN), lambda l, n: (l, 0, n)),
            pl.BlockSpec((None, 1, ADA_TN), lambda l, n: (l, 0, n)),
        ],
        out_specs=pl.BlockSpec((None, ADA_ROWS, ADA_TN), lambda l, n: (l, 0, n)),
        out_shape=jax.ShapeDtypeStruct((DEPTH, ADA_ROWS, n_out), F32),
        compiler_params=pltpu.CompilerParams(
            dimension_semantics=("arbitrary", "arbitrary"), vmem_limit_bytes=VMEM_LIMIT),
        name="ada_rows",
    )(c_all, w_ada, b_ada.reshape(DEPTH, 1, n_out))


def _in_proj_kernel(xl_ref, xc_ref, mod_ref, ng_ref, w_in_ref, qg_ref, kg_ref, cqg_ref, ckvg_ref,
                    w_uq_ref, w_ukv_ref, tab_ref,
                    qa_ref, ka_ref, va_ref, qm_ref, km_ref, vm_ref, g_ref):
    x = _stream_block(xl_ref, xc_ref)
    h = _rms(x) * ng_ref[...] * (1.0 + mod_ref[1:2, :]) + mod_ref[0:1, :]
    p = jnp.dot(h.astype(BF16), w_in_ref[...], preferred_element_type=F32)

    q_gain = qg_ref[...] * (HEAD_DIM ** -0.5 * LOG2_E)
    for i in range(GQA_HEADS):
        c0 = COL_Q + i * HEAD_DIM
        qh = _rms(p[:, c0:c0 + HEAD_DIM]) * q_gain
        qa_ref[:, i * HEAD_DIM:(i + 1) * HEAD_DIM] = _rot_a(qh, tab_ref).astype(BF16)
    k_gain = kg_ref[...]
    for i in range(GQA_KV_HEADS):
        c0 = COL_K + i * HEAD_DIM
        kh = _rms(p[:, c0:c0 + HEAD_DIM]) * k_gain
        ka_ref[:, i * HEAD_DIM:(i + 1) * HEAD_DIM] = _rot_a(kh, tab_ref).astype(BF16)
    ones = jnp.ones((TM, HEAD_DIM), BF16)
    for i in range(GQA_KV_HEADS):
        c0 = COL_V + i * HEAD_DIM
        va_ref[:, i * V_EXT:i * V_EXT + HEAD_DIM] = p[:, c0:c0 + HEAD_DIM].astype(BF16)
        va_ref[:, i * V_EXT + HEAD_DIM:(i + 1) * V_EXT] = ones

    g_ref[:, :GQA_WIDTH] = _silu(p[:, COL_GG:COL_GG + GQA_WIDTH]).astype(BF16)
    g_ref[:, GQA_WIDTH:] = _silu(p[:, COL_GM:COL_GM + MLA_WIDTH]).astype(BF16)

    cq = _rms(p[:, COL_CQ:COL_CQ + MLA_Q_RANK]) * cqg_ref[...]
    qm = jnp.dot(cq.astype(BF16), w_uq_ref[...], preferred_element_type=F32)
    m_scale = MLA_QK ** -0.5 * LOG2_E
    for i in range(MLA_HEADS):
        c0 = i * MLA_DK
        qm_ref[:, c0:c0 + MLA_NOPE] = (qm[:, c0:c0 + MLA_NOPE] * m_scale).astype(BF16)
        qm_ref[:, c0 + MLA_NOPE:c0 + MLA_DK] = _rot_m(
            qm[:, c0 + MLA_NOPE:c0 + MLA_DK], tab_ref, m_scale).astype(BF16)

    ckv = _rms(p[:, COL_CKV:COL_CKV + MLA_KV_RANK]) * ckvg_ref[...]
    kvm = jnp.dot(ckv.astype(BF16), w_ukv_ref[...], preferred_element_type=F32)
    kr = _rot_m(p[:, COL_KR:COL_KR + LANES], tab_ref).astype(BF16)
    for i in range(MLA_HEADS):
        c0 = i * (MLA_NOPE + MLA_V)
        km_ref[:, i * MLA_DK:i * MLA_DK + MLA_NOPE] = kvm[:, c0:c0 + MLA_NOPE].astype(BF16)
        km_ref[:, i * MLA_DK + MLA_NOPE:(i + 1) * MLA_DK] = kr
        vm_ref[:, i * V_EXT:i * V_EXT + MLA_V] = kvm[:, c0 + MLA_NOPE:c0 + MLA_NOPE + MLA_V].astype(BF16)
        vm_ref[:, i * V_EXT + MLA_V:(i + 1) * V_EXT] = ones


def _in_proj(layer, stream, ctx_blk, mod, params, tab):
    n_b = stream[0].shape[0]
    tok = lambda width: pl.BlockSpec((None, TM, width), lambda b, j: (b, j, 0))
    lay = lambda *shape: pl.BlockSpec((None,) + shape, lambda b, j: (layer,) + (0,) * len(shape),
                                      pipeline_mode=pl.Buffered(1))
    widths = (GQA_WIDTH, GQA_KV_WIDTH, GQA_KV_HEADS * V_EXT,
              MLA_QK_WIDTH, MLA_QK_WIDTH, MLA_HEADS * V_EXT, D_MIX)
    return pl.pallas_call(
        _in_proj_kernel,
        grid=(n_b, N_BLK),
        in_specs=_stream_specs(ctx_blk) + [
            pl.BlockSpec((None, None, 3, D_MODEL), lambda b, j: (layer, _mod_row(b, j), 0, 0)),
            lay(1, D_MODEL),
            lay(D_MODEL, D_IN_PACKED),
            lay(1, HEAD_DIM), lay(1, HEAD_DIM), lay(1, MLA_Q_RANK), lay(1, MLA_KV_RANK),
            lay(MLA_Q_RANK, MLA_QK_WIDTH),
            lay(MLA_KV_RANK, MLA_HEADS * (MLA_NOPE + MLA_V)),
            pl.BlockSpec((TM, TAB_WIDTH), lambda b, j: (j, 0)),
        ],
        out_specs=[tok(w) for w in widths],
        out_shape=[jax.ShapeDtypeStruct((n_b, T_TOK, w), BF16) for w in widths],
        compiler_params=pltpu.CompilerParams(
            dimension_semantics=("arbitrary", "arbitrary"), vmem_limit_bytes=VMEM_LIMIT),
        name="in_proj",
    )(*stream, mod, *params, tab)


def _attend(q_ref, k_ref, v_ref, g_ref, o_ref, *, kv_heads, group, dk, dv):
    for p in range(kv_heads):
        k = k_ref[:, p * dk:(p + 1) * dk]
        v = v_ref[:, p * V_EXT:(p + 1) * V_EXT]
        for r0 in range(0, q_ref.shape[0], TM):
            for i in range(group):
                hd = p * group + i
                q = q_ref[r0:r0 + TM, hd * dk:(hd + 1) * dk]
                s = lax.dot_general(q, k, (((1,), (1,)), ((), ())), preferred_element_type=F32)
                e = jnp.exp2(s - jnp.max(s, axis=-1, keepdims=True))
                o = jnp.dot(e.astype(BF16), v, preferred_element_type=F32)
                gate = g_ref[r0:r0 + TM, hd * dv:(hd + 1) * dv].astype(F32)
                o_ref[r0:r0 + TM, hd * dv:(hd + 1) * dv] = (
                    o[:, :dv] * (gate / o[:, dv:])).astype(BF16)


_GQA_CFG = dict(kv_heads=GQA_KV_HEADS, group=GQA_GROUP, dk=HEAD_DIM, dv=HEAD_DIM)


def _attn_gqa(q, k, v, g):
    n_b = q.shape[0]
    return pl.pallas_call(
        functools.partial(_attend, **_GQA_CFG),
        grid=(n_b, SEQ // TQ_GQA),
        in_specs=[
            pl.BlockSpec((None, TQ_GQA, GQA_WIDTH), lambda b, j: (b, j, 0)),
            pl.BlockSpec((None, T_TOK, GQA_KV_WIDTH), lambda b, j: (b, 0, 0)),
            pl.BlockSpec((None, T_TOK, GQA_KV_HEADS * V_EXT), lambda b, j: (b, 0, 0)),
            pl.BlockSpec((None, TQ_GQA, GQA_WIDTH), lambda b, j: (b, j, 0)),
        ],
        out_specs=pl.BlockSpec((None, TQ_GQA, GQA_WIDTH), lambda b, j: (b, j, 0)),
        out_shape=jax.ShapeDtypeStruct((n_b, SEQ, GQA_WIDTH), BF16),
        compiler_params=pltpu.CompilerParams(
            dimension_semantics=("arbitrary", "arbitrary"), vmem_limit_bytes=VMEM_LIMIT),
        name="attn_gqa",
    )(q, k, v, g)


def _attn_mla(q, k, v, g):
    n_b = q.shape[0]
    hps = MLA_HEADS_PER_STEP
    g_col0 = GQA_WIDTH // (hps * MLA_V)
    return pl.pallas_call(
        functools.partial(_attend, kv_heads=hps, group=1, dk=MLA_DK, dv=MLA_V),
        grid=(n_b, MLA_HEADS // hps, SEQ // TQ_MLA),
        in_specs=[
            pl.BlockSpec((None, TQ_MLA, hps * MLA_DK), lambda b, hg, j: (b, j, hg)),
            pl.BlockSpec((None, T_TOK, hps * MLA_DK), lambda b, hg, j: (b, 0, hg)),
            pl.BlockSpec((None, T_TOK, hps * V_EXT), lambda b, hg, j: (b, 0, hg)),
            pl.BlockSpec((None, TQ_MLA, hps * MLA_V), lambda b, hg, j: (b, j, g_col0 + hg)),
        ],
        out_specs=pl.BlockSpec((None, TQ_MLA, hps * MLA_V), lambda b, hg, j: (b, j, hg)),
        out_shape=jax.ShapeDtypeStruct((n_b, SEQ, MLA_WIDTH), BF16),
        compiler_params=pltpu.CompilerParams(
            dimension_semantics=("arbitrary", "arbitrary", "arbitrary"), vmem_limit_bytes=VMEM_LIMIT),
        name="attn_mla",
    )(q, k, v, g)


def _attn_ctx_kernel(qa_ref, ka_ref, va_ref, qm_ref, km_ref, vm_ref, g_ref, oa_ref, om_ref):
    _attend(qa_ref, ka_ref, va_ref, g_ref.at[:, :GQA_WIDTH], oa_ref, **_GQA_CFG)
    _attend(qm_ref, km_ref, vm_ref, g_ref.at[:, GQA_WIDTH:], om_ref,
            kv_heads=MLA_HEADS, group=1, dk=MLA_DK, dv=MLA_V)


def _attn_ctx(qa, ka, va, qm, km, vm, g):
    n_b = qa.shape[0]
    ctx_rows = lambda width: pl.BlockSpec((None, CTX_LEN, width), lambda b: (b, SEQ // CTX_LEN, 0))
    out = lambda width: pl.BlockSpec((None, CTX_LEN, width), lambda b: (b, 0, 0))
    return pl.pallas_call(
        _attn_ctx_kernel,
        grid=(n_b,),
        in_specs=[ctx_rows(GQA_WIDTH), ctx_rows(GQA_KV_WIDTH), ctx_rows(GQA_KV_HEADS * V_EXT),
                  ctx_rows(MLA_QK_WIDTH), ctx_rows(MLA_QK_WIDTH), ctx_rows(MLA_HEADS * V_EXT),
                  ctx_rows(D_MIX)],
        out_specs=[out(GQA_WIDTH), out(MLA_WIDTH)],
        out_shape=[jax.ShapeDtypeStruct((n_b, CTX_LEN, GQA_WIDTH), BF16),
                   jax.ShapeDtypeStruct((n_b, CTX_LEN, MLA_WIDTH), BF16)],
        compiler_params=pltpu.CompilerParams(
            dimension_semantics=("arbitrary",), vmem_limit_bytes=VMEM_LIMIT),
        name="attn_ctx",
    )(qa, ka, va, qm, km, vm, g)


def _out_proj_kernel(*refs, last):
    if last:
        ya_ref, ym_ref, w_ref, x_ref, mod_ref, fg_ref, o_ref = refs
        a, m, x = ya_ref[...], ym_ref[...], x_ref[...]
    else:
        ya_ref, ym_ref, yca_ref, ycm_ref, w_ref, xl_ref, xc_ref, mod_ref, o_ref = refs
        is_ctx = pl.program_id(1) == N_LAT_BLK
        a = jnp.where(is_ctx, yca_ref[...], ya_ref[...])
        m = jnp.where(is_ctx, ycm_ref[...], ym_ref[...])
        x = _stream_block(xl_ref, xc_ref)
    y = jnp.concatenate([a, m], axis=-1)
    x = x + mod_ref[2:3, :] * jnp.dot(y, w_ref[...], preferred_element_type=F32)
    o_ref[...] = _rms(x) * fg_ref[...] if last else x


def _out_proj(layer, y_lat, y_ctx, w_out_b, stream, ctx_blk, mod, final_g):
    n_b = y_lat[0].shape[0]
    last = y_ctx is None
    lat = lambda width: pl.BlockSpec((None, TM, width),
                                     lambda b, j: (b, jnp.minimum(j, N_LAT_BLK - 1), 0))
    ctx = lambda width: pl.BlockSpec((None, CTX_LEN, width), lambda b, j: (b, 0, 0))
    w_spec = pl.BlockSpec((None, D_MIX, D_MODEL), lambda b, j: (layer, 0, 0))
    mod_spec = pl.BlockSpec((None, None, 3, D_MODEL), lambda b, j: (layer, _mod_row(b, j), 0, 0))
    if last:
        n_blk = N_LAT_BLK
        in_specs = [lat(GQA_WIDTH), lat(MLA_WIDTH), w_spec, _stream_specs(ctx_blk)[0], mod_spec,
                    pl.BlockSpec((1, D_MODEL), lambda b, j: (0, 0))]
        args = (*y_lat, w_out_b, stream[0], mod, final_g.reshape(1, D_MODEL))
    else:
        n_blk = N_BLK
        in_specs = [lat(GQA_WIDTH), lat(MLA_WIDTH), ctx(GQA_WIDTH), ctx(MLA_WIDTH), w_spec,
                    *_stream_specs(ctx_blk), mod_spec]
        args = (*y_lat, *y_ctx, w_out_b, *stream, mod)
    return pl.pallas_call(
        functools.partial(_out_proj_kernel, last=last),
        grid=(n_b, n_blk),
        in_specs=in_specs,
        out_specs=pl.BlockSpec((None, TM, D_MODEL), lambda b, j: (b, j, 0)),
        out_shape=jax.ShapeDtypeStruct((n_b, n_blk * TM, D_MODEL), F32),
        compiler_params=pltpu.CompilerParams(
            dimension_semantics=("arbitrary", "arbitrary"), vmem_limit_bytes=VMEM_LIMIT),
        name="out_proj",
    )(*args)


def _pack_w_in_kernel(w_ref, o_ref):
    o_ref[:, :COL_KR] = w_ref[:, :COL_KR].astype(BF16)
    kr = w_ref[:, COL_KR:COL_KR + LANES]
    lane = lax.broadcasted_iota(jnp.int32, kr.shape, 1)
    o_ref[:, COL_KR:COL_GM] = jnp.where(lane < MLA_ROPE, kr, 0.0).astype(BF16)
    o_ref[:, COL_GM:] = w_ref[:, COL_KR + MLA_ROPE:].astype(BF16)


def _pack_w_in(w_in):
    return pl.pallas_call(
        _pack_w_in_kernel,
        grid=(DEPTH, D_MODEL // PACK_ROWS),
        in_specs=[pl.BlockSpec((None, PACK_ROWS, D_IN), lambda l, r: (l, r, 0))],
        out_specs=pl.BlockSpec((None, PACK_ROWS, D_IN_PACKED), lambda l, r: (l, r, 0)),
        out_shape=jax.ShapeDtypeStruct((DEPTH, D_MODEL, D_IN_PACKED), BF16),
        compiler_params=pltpu.CompilerParams(
            dimension_semantics=("arbitrary", "arbitrary"), vmem_limit_bytes=VMEM_LIMIT),
        name="pack_w_in",
    )(w_in)


def _pack_w_uq(w_uq):
    w = w_uq.reshape(DEPTH, MLA_Q_RANK, MLA_HEADS, MLA_QK)
    w = jnp.pad(w, ((0, 0), (0, 0), (0, 0), (0, MLA_DK - MLA_QK)))
    return w.reshape(DEPTH, MLA_Q_RANK, MLA_QK_WIDTH).astype(BF16)


def _rope_table():
    rows = SEQ // GRID_W
    row = jnp.repeat(jnp.arange(rows, dtype=F32), GRID_W)
    col = jnp.tile(jnp.arange(GRID_W, dtype=F32), rows)

    def cos_sin(rot_dim):
        n_freq = rot_dim // 4
        inv = ROPE_THETA ** (-jnp.arange(n_freq, dtype=F32) / n_freq)
        ang = jnp.concatenate([row[:, None] * inv[None], col[:, None] * inv[None]], axis=-1)
        return jnp.cos(ang), jnp.sin(ang)

    cos, sin = cos_sin(HEAD_DIM)
    cos_m, sin_m = cos_sin(MLA_ROPE)
    z32 = jnp.zeros_like(sin_m)
    z64 = jnp.zeros_like(sin)
    lat = jnp.concatenate([
        cos, cos,
        -sin, sin,
        cos_m, cos_m, z64,
        -sin_m, z32, z64,
        z32, sin_m, z64,
    ], axis=-1)
    ctx_row = jnp.concatenate([jnp.ones((LANES,), F32), jnp.zeros((LANES,), F32),
                               jnp.ones((LANES,), F32), jnp.zeros((2 * LANES,), F32)])
    return jnp.concatenate([lat, jnp.broadcast_to(ctx_row, (CTX_LEN, TAB_WIDTH))], axis=0)


def kernel(x, c, ctx, c_ctx, w_ada, b_ada, norm_g, w_in, q_gain, k_gain, cq_gain, ckv_gain,
           w_uq, w_ukv, w_out, final_g):
    n_b = x.shape[0]
    assert x.shape == (n_b, SEQ, D_MODEL) and ctx.shape == (n_b, CTX_LEN, D_MODEL)
    assert n_b == ADA_CTX_ROW and w_ada.shape[0] == DEPTH and w_in.shape[-1] == D_IN

    c_all = jnp.concatenate(
        [c, c_ctx[None], jnp.zeros((ADA_ROWS - n_b - 1, D_MODEL), F32)], axis=0)
    mod = _ada_rows(c_all, w_ada, b_ada).reshape(DEPTH, ADA_ROWS, 3, D_MODEL)

    row3 = lambda a: a.reshape(DEPTH, 1, a.shape[-1])
    params = (row3(norm_g), _pack_w_in(w_in), row3(q_gain), row3(k_gain), row3(cq_gain),
              row3(ckv_gain), _pack_w_uq(w_uq), w_ukv.astype(BF16))
    w_out_b = w_out.astype(BF16)
    tab = _rope_table()

    stream, ctx_blk = (x, ctx), 0
    for layer in range(DEPTH):
        last = layer == DEPTH - 1
        qa, ka, va, qm, km, vm, g = _in_proj(layer, stream, ctx_blk, mod, params, tab)
        y_lat = (_attn_gqa(qa, ka, va, g), _attn_mla(qm, km, vm, g))
        y_ctx = None if last else _attn_ctx(qa, ka, va, qm, km, vm, g)
        xs = _out_proj(layer, y_lat, y_ctx, w_out_b, stream, ctx_blk, mod, final_g)
        stream, ctx_blk = (xs, xs), SEQ // CTX_LEN
    return xs
```

```python
import functools

import jax
import jax.numpy as jnp
from jax import lax
from jax.experimental import pallas as pl
from jax.experimental.pallas import tpu as pltpu

F32 = jnp.float32
BF16 = jnp.bfloat16

D_MODEL = 2048
SEQ = 2048
CTX_LEN = 256
T_TOK = SEQ + CTX_LEN
GRID_W = 64
ROPE_THETA = 10000.0
EPS = 1e-6
DEPTH = 4
LOG2_E = 1.4426950408889634

LANES = 128
HEAD_DIM = 128
GQA_HEADS = 8
GQA_KV_HEADS = 2
GQA_GROUP = GQA_HEADS // GQA_KV_HEADS
GQA_WIDTH = GQA_HEADS * HEAD_DIM
GQA_KV_WIDTH = GQA_KV_HEADS * HEAD_DIM

MLA_HEADS = 8
MLA_NOPE = 128
MLA_ROPE = 64
MLA_QK = MLA_NOPE + MLA_ROPE
MLA_V = 128
MLA_Q_RANK = 512
MLA_KV_RANK = 512
MLA_WIDTH = MLA_HEADS * MLA_V
D_MIX = GQA_WIDTH + MLA_WIDTH
MLA_DK = MLA_NOPE + LANES
MLA_QK_WIDTH = MLA_HEADS * MLA_DK
V_EXT = 2 * HEAD_DIM

COL_Q = 0
COL_K = COL_Q + GQA_WIDTH
COL_V = COL_K + GQA_KV_WIDTH
COL_GG = COL_V + GQA_KV_WIDTH
COL_CQ = COL_GG + GQA_WIDTH
COL_CKV = COL_CQ + MLA_Q_RANK
COL_KR = COL_CKV + MLA_KV_RANK
COL_GM = COL_KR + MLA_ROPE
D_IN = COL_GM + MLA_WIDTH

TAB_COS_A, TAB_SIN_A, TAB_COS_M, TAB_SIN_M_LO, TAB_SIN_M_HI = range(5)
TAB_WIDTH = 5 * LANES

ADA_ROWS = 8
ADA_CTX_ROW = 4
ADA_TN = 1024

TM = 256
TQ_GQA = 512
TQ_MLA = 1024
N_BLK = T_TOK // TM
N_LAT_BLK = SEQ // TM
MLA_HEADS_PER_STEP = 4

VMEM_LIMIT = 56 * 1024 * 1024

assert CTX_LEN == TM and SEQ % TQ_GQA == 0 and SEQ % TQ_MLA == 0


def _silu(x):
    return x / (1.0 + jnp.exp(-x))


def _rms(x):
    return x * lax.rsqrt(jnp.mean(x * x, axis=-1, keepdims=True) + EPS)


def _tab(tab_ref, col):
    return tab_ref[:, col * LANES:(col + 1) * LANES]


def _rot_a(x, tab_ref):
    return x * _tab(tab_ref, TAB_COS_A) + pltpu.roll(x, HEAD_DIM // 2, 1) * _tab(tab_ref, TAB_SIN_A)


def _rot_m(x, tab_ref, scale=None):
    half = MLA_ROPE // 2
    y = (x * _tab(tab_ref, TAB_COS_M)
         + pltpu.roll(x, LANES - half, 1) * _tab(tab_ref, TAB_SIN_M_LO)
         + pltpu.roll(x, half, 1) * _tab(tab_ref, TAB_SIN_M_HI))
    return y if scale is None else y * scale


def _mod_row(b, j):
    return jnp.where(j == N_LAT_BLK, ADA_CTX_ROW, b)


def _stream_specs(ctx_blk):
    return [pl.BlockSpec((None, TM, D_MODEL), lambda b, j: (b, jnp.minimum(j, N_LAT_BLK - 1), 0)),
            pl.BlockSpec((None, CTX_LEN, D_MODEL), lambda b, j: (b, ctx_blk, 0))]


def _stream_block(xl_ref, xc_ref):
    return jnp.where(pl.program_id(1) == N_LAT_BLK, xc_ref[...], xl_ref[...])


def _ada_kernel(c_ref, w_ref, b_ref, o_ref):
    s = _silu(c_ref[...]).astype(BF16)
    w = w_ref[...].astype(BF16)
    o_ref[...] = jnp.dot(s, w, preferred_element_type=F32) + b_ref[...]


def _ada_rows(c_all, w_ada, b_ada):
    n_out = 3 * D_MODEL
    return pl.pallas_call(
        _ada_kernel,
        grid=(DEPTH, n_out // ADA_TN),
        in_specs=[
            pl.BlockSpec((ADA_ROWS, D_MODEL), lambda l, n: (0, 0)),
            pl.BlockSpec((None, D_MODEL, ADA_TN), lambda l, n: (l, 0, n)),
            pl.BlockSpec((None, 1, ADA_TN), lambda l, n: (l, 0, n)),
        ],
        out_specs=pl.BlockSpec((None, ADA_ROWS, ADA_TN), lambda l, n: (l, 0, n)),
        out_shape=jax.ShapeDtypeStruct((DEPTH, ADA_ROWS, n_out), F32),
        compiler_params=pltpu.CompilerParams(
            dimension_semantics=("arbitrary", "arbitrary"), vmem_limit_bytes=VMEM_LIMIT),
        name="ada_rows",
    )(c_all, w_ada, b_ada.reshape(DEPTH, 1, n_out))


def _in_proj_kernel(xl_ref, xc_ref, mod_ref, ng_ref, w_in_ref, qg_ref, kg_ref, cqg_ref, ckvg_ref,
                    w_uq_ref, w_ukv_ref, tab_ref,
                    qa_ref, ka_ref, va_ref, qm_ref, km_ref, vm_ref, g_ref):
    x = _stream_block(xl_ref, xc_ref)
    h = _rms(x) * ng_ref[...] * (1.0 + mod_ref[1:2, :]) + mod_ref[0:1, :]
    hb = h.astype(BF16)
    proj = lambda r0, r1: lax.dot_general(hb, w_in_ref[r0:r1, :], (((1,), (1,)), ((), ())),
                                          preferred_element_type=F32)
    p = proj(0, COL_KR)
    kr_in = proj(COL_KR, COL_KR + LANES)
    kr_in = jnp.where(lax.broadcasted_iota(jnp.int32, kr_in.shape, 1) < MLA_ROPE, kr_in, 0.0)
    g_mla = proj(COL_GM, D_IN)

    q_gain = qg_ref[...] * (HEAD_DIM ** -0.5 * LOG2_E)
    for i in range(GQA_HEADS):
        c0 = COL_Q + i * HEAD_DIM
        qh = _rms(p[:, c0:c0 + HEAD_DIM]) * q_gain
        qa_ref[:, i * HEAD_DIM:(i + 1) * HEAD_DIM] = _rot_a(qh, tab_ref).astype(BF16)
    k_gain = kg_ref[...]
    for i in range(GQA_KV_HEADS):
        c0 = COL_K + i * HEAD_DIM
        kh = _rms(p[:, c0:c0 + HEAD_DIM]) * k_gain
        ka_ref[:, i * HEAD_DIM:(i + 1) * HEAD_DIM] = _rot_a(kh, tab_ref).astype(BF16)
    ones = jnp.ones((TM, HEAD_DIM), BF16)
    for i in range(GQA_KV_HEADS):
        c0 = COL_V + i * HEAD_DIM
        va_ref[:, i * V_EXT:i * V_EXT + HEAD_DIM] = p[:, c0:c0 + HEAD_DIM].astype(BF16)
        va_ref[:, i * V_EXT + HEAD_DIM:(i + 1) * V_EXT] = ones

    g_ref[:, :GQA_WIDTH] = _silu(p[:, COL_GG:COL_GG + GQA_WIDTH]).astype(BF16)
    g_ref[:, GQA_WIDTH:] = _silu(g_mla).astype(BF16)

    cq = _rms(p[:, COL_CQ:COL_CQ + MLA_Q_RANK]) * cqg_ref[...]
    qm = jnp.dot(cq.astype(BF16), w_uq_ref[...], preferred_element_type=F32)
    m_scale = MLA_QK ** -0.5 * LOG2_E
    for i in range(MLA_HEADS):
        c0 = i * MLA_DK
        qm_ref[:, c0:c0 + MLA_NOPE] = (qm[:, c0:c0 + MLA_NOPE] * m_scale).astype(BF16)
        qm_ref[:, c0 + MLA_NOPE:c0 + MLA_DK] = _rot_m(
            qm[:, c0 + MLA_NOPE:c0 + MLA_DK], tab_ref, m_scale).astype(BF16)

    ckv = _rms(p[:, COL_CKV:COL_CKV + MLA_KV_RANK]) * ckvg_ref[...]
    kvm = jnp.dot(ckv.astype(BF16), w_ukv_ref[...], preferred_element_type=F32)
    kr = _rot_m(kr_in, tab_ref).astype(BF16)
    for i in range(MLA_HEADS):
        c0 = i * (MLA_NOPE + MLA_V)
        km_ref[:, i * MLA_DK:i * MLA_DK + MLA_NOPE] = kvm[:, c0:c0 + MLA_NOPE].astype(BF16)
        km_ref[:, i * MLA_DK + MLA_NOPE:(i + 1) * MLA_DK] = kr
        vm_ref[:, i * V_EXT:i * V_EXT + MLA_V] = kvm[:, c0 + MLA_NOPE:c0 + MLA_NOPE + MLA_V].astype(BF16)
        vm_ref[:, i * V_EXT + MLA_V:(i + 1) * V_EXT] = ones


def _in_proj(layer, stream, ctx_blk, mod, params, tab):
    n_b = stream[0].shape[0]
    tok = lambda width: pl.BlockSpec((None, TM, width), lambda b, j: (b, j, 0))
    lay = lambda *shape: pl.BlockSpec((None,) + shape, lambda b, j: (layer,) + (0,) * len(shape),
                                      pipeline_mode=pl.Buffered(1))
    widths = (GQA_WIDTH, GQA_KV_WIDTH, GQA_KV_HEADS * V_EXT,
              MLA_QK_WIDTH, MLA_QK_WIDTH, MLA_HEADS * V_EXT, D_MIX)
    return pl.pallas_call(
        _in_proj_kernel,
        grid=(n_b, N_BLK),
        in_specs=_stream_specs(ctx_blk) + [
            pl.BlockSpec((None, None, 3, D_MODEL), lambda b, j: (layer, _mod_row(b, j), 0, 0)),
            lay(1, D_MODEL),
            lay(D_IN, D_MODEL),
            lay(1, HEAD_DIM), lay(1, HEAD_DIM), lay(1, MLA_Q_RANK), lay(1, MLA_KV_RANK),
            lay(MLA_Q_RANK, MLA_QK_WIDTH),
            lay(MLA_KV_RANK, MLA_HEADS * (MLA_NOPE + MLA_V)),
            pl.BlockSpec((TM, TAB_WIDTH), lambda b, j: (j, 0)),
        ],
        out_specs=[tok(w) for w in widths],
        out_shape=[jax.ShapeDtypeStruct((n_b, T_TOK, w), BF16) for w in widths],
        compiler_params=pltpu.CompilerParams(
            dimension_semantics=("arbitrary", "arbitrary"), vmem_limit_bytes=VMEM_LIMIT),
        name="in_proj",
    )(*stream, mod, *params, tab)


def _attend(q_ref, k_ref, v_ref, g_ref, o_ref, *, kv_heads, group, dk, dv):
    for p in range(kv_heads):
        k = k_ref[:, p * dk:(p + 1) * dk]
        v = v_ref[:, p * V_EXT:(p + 1) * V_EXT]
        for r0 in range(0, q_ref.shape[0], TM):
            for i in range(group):
                hd = p * group + i
                q = q_ref[r0:r0 + TM, hd * dk:(hd + 1) * dk]
                s = lax.dot_general(q, k, (((1,), (1,)), ((), ())), preferred_element_type=F32)
                e = jnp.exp2(s - jnp.max(s, axis=-1, keepdims=True))
                o = jnp.dot(e.astype(BF16), v, preferred_element_type=F32)
                gate = g_ref[r0:r0 + TM, hd * dv:(hd + 1) * dv].astype(F32)
                o_ref[r0:r0 + TM, hd * dv:(hd + 1) * dv] = (
                    o[:, :dv] * (gate / o[:, dv:])).astype(BF16)


_GQA_CFG = dict(kv_heads=GQA_KV_HEADS, group=GQA_GROUP, dk=HEAD_DIM, dv=HEAD_DIM)


def _attn_gqa(q, k, v, g):
    n_b = q.shape[0]
    return pl.pallas_call(
        functools.partial(_attend, **_GQA_CFG),
        grid=(n_b, SEQ // TQ_GQA),
        in_specs=[
            pl.BlockSpec((None, TQ_GQA, GQA_WIDTH), lambda b, j: (b, j, 0)),
            pl.BlockSpec((None, T_TOK, GQA_KV_WIDTH), lambda b, j: (b, 0, 0)),
            pl.BlockSpec((None, T_TOK, GQA_KV_HEADS * V_EXT), lambda b, j: (b, 0, 0)),
            pl.BlockSpec((None, TQ_GQA, GQA_WIDTH), lambda b, j: (b, j, 0)),
        ],
        out_specs=pl.BlockSpec((None, TQ_GQA, GQA_WIDTH), lambda b, j: (b, j, 0)),
        out_shape=jax.ShapeDtypeStruct((n_b, SEQ, GQA_WIDTH), BF16),
        compiler_params=pltpu.CompilerParams(
            dimension_semantics=("arbitrary", "arbitrary"), vmem_limit_bytes=VMEM_LIMIT),
        name="attn_gqa",
    )(q, k, v, g)


def _attn_mla(q, k, v, g):
    n_b = q.shape[0]
    hps = MLA_HEADS_PER_STEP
    g_col0 = GQA_WIDTH // (hps * MLA_V)
    return pl.pallas_call(
        functools.partial(_attend, kv_heads=hps, group=1, dk=MLA_DK, dv=MLA_V),
        grid=(n_b, MLA_HEADS // hps, SEQ // TQ_MLA),
        in_specs=[
            pl.BlockSpec((None, TQ_MLA, hps * MLA_DK), lambda b, hg, j: (b, j, hg)),
            pl.BlockSpec((None, T_TOK, hps * MLA_DK), lambda b, hg, j: (b, 0, hg)),
            pl.BlockSpec((None, T_TOK, hps * V_EXT), lambda b, hg, j: (b, 0, hg)),
            pl.BlockSpec((None, TQ_MLA, hps * MLA_V), lambda b, hg, j: (b, j, g_col0 + hg)),
        ],
        out_specs=pl.BlockSpec((None, TQ_MLA, hps * MLA_V), lambda b, hg, j: (b, j, hg)),
        out_shape=jax.ShapeDtypeStruct((n_b, SEQ, MLA_WIDTH), BF16),
        compiler_params=pltpu.CompilerParams(
            dimension_semantics=("arbitrary", "arbitrary", "arbitrary"), vmem_limit_bytes=VMEM_LIMIT),
        name="attn_mla",
    )(q, k, v, g)


def _attn_ctx_kernel(qa_ref, ka_ref, va_ref, qm_ref, km_ref, vm_ref, g_ref, oa_ref, om_ref):
    _attend(qa_ref, ka_ref, va_ref, g_ref.at[:, :GQA_WIDTH], oa_ref, **_GQA_CFG)
    _attend(qm_ref, km_ref, vm_ref, g_ref.at[:, GQA_WIDTH:], om_ref,
            kv_heads=MLA_HEADS, group=1, dk=MLA_DK, dv=MLA_V)


def _attn_ctx(qa, ka, va, qm, km, vm, g):
    n_b = qa.shape[0]
    ctx_rows = lambda width: pl.BlockSpec((None, CTX_LEN, width), lambda b: (b, SEQ // CTX_LEN, 0))
    out = lambda width: pl.BlockSpec((None, CTX_LEN, width), lambda b: (b, 0, 0))
    return pl.pallas_call(
        _attn_ctx_kernel,
        grid=(n_b,),
        in_specs=[ctx_rows(GQA_WIDTH), ctx_rows(GQA_KV_WIDTH), ctx_rows(GQA_KV_HEADS * V_EXT),
                  ctx_rows(MLA_QK_WIDTH), ctx_rows(MLA_QK_WIDTH), ctx_rows(MLA_HEADS * V_EXT),
                  ctx_rows(D_MIX)],
        out_specs=[out(GQA_WIDTH), out(MLA_WIDTH)],
        out_shape=[jax.ShapeDtypeStruct((n_b, CTX_LEN, GQA_WIDTH), BF16),
                   jax.ShapeDtypeStruct((n_b, CTX_LEN, MLA_WIDTH), BF16)],
        compiler_params=pltpu.CompilerParams(
            dimension_semantics=("arbitrary",), vmem_limit_bytes=VMEM_LIMIT),
        name="attn_ctx",
    )(qa, ka, va, qm, km, vm, g)


def _out_proj_kernel(*refs, last):
    if last:
        ya_ref, ym_ref, w_ref, x_ref, mod_ref, fg_ref, o_ref = refs
        a, m, x = ya_ref[...], ym_ref[...], x_ref[...]
    else:
        ya_ref, ym_ref, yca_ref, ycm_ref, w_ref, xl_ref, xc_ref, mod_ref, o_ref = refs
        is_ctx = pl.program_id(1) == N_LAT_BLK
        a = jnp.where(is_ctx, yca_ref[...], ya_ref[...])
        m = jnp.where(is_ctx, ycm_ref[...], ym_ref[...])
        x = _stream_block(xl_ref, xc_ref)
    y = jnp.concatenate([a, m], axis=-1)
    x = x + mod_ref[2:3, :] * jnp.dot(y, w_ref[...], preferred_element_type=F32)
    o_ref[...] = _rms(x) * fg_ref[...] if last else x


def _out_proj(layer, y_lat, y_ctx, w_out_b, stream, ctx_blk, mod, final_g):
    n_b = y_lat[0].shape[0]
    last = y_ctx is None
    lat = lambda width: pl.BlockSpec((None, TM, width),
                                     lambda b, j: (b, jnp.minimum(j, N_LAT_BLK - 1), 0))
    ctx = lambda width: pl.BlockSpec((None, CTX_LEN, width), lambda b, j: (b, 0, 0))
    w_spec = pl.BlockSpec((None, D_MIX, D_MODEL), lambda b, j: (layer, 0, 0))
    mod_spec = pl.BlockSpec((None, None, 3, D_MODEL), lambda b, j: (layer, _mod_row(b, j), 0, 0))
    if last:
        n_blk = N_LAT_BLK
        in_specs = [lat(GQA_WIDTH), lat(MLA_WIDTH), w_spec, _stream_specs(ctx_blk)[0], mod_spec,
                    pl.BlockSpec((1, D_MODEL), lambda b, j: (0, 0))]
        args = (*y_lat, w_out_b, stream[0], mod, final_g.reshape(1, D_MODEL))
    else:
        n_blk = N_BLK
        in_specs = [lat(GQA_WIDTH), lat(MLA_WIDTH), ctx(GQA_WIDTH), ctx(MLA_WIDTH), w_spec,
                    *_stream_specs(ctx_blk), mod_spec]
        args = (*y_lat, *y_ctx, w_out_b, *stream, mod)
    return pl.pallas_call(
        functools.partial(_out_proj_kernel, last=last),
        grid=(n_b, n_blk),
        in_specs=in_specs,
        out_specs=pl.BlockSpec((None, TM, D_MODEL), lambda b, j: (b, j, 0)),
        out_shape=jax.ShapeDtypeStruct((n_b, n_blk * TM, D_MODEL), F32),
        compiler_params=pltpu.CompilerParams(
            dimension_semantics=("arbitrary", "arbitrary"), vmem_limit_bytes=VMEM_LIMIT),
        name="out_proj",
    )(*args)


def _pack_w_uq(w_uq):
    w = w_uq.reshape(DEPTH, MLA_Q_RANK, MLA_HEADS, MLA_QK)
    w = jnp.pad(w, ((0, 0), (0, 0), (0, 0), (0, MLA_DK - MLA_QK)))
    return w.reshape(DEPTH, MLA_Q_RANK, MLA_QK_WIDTH).astype(BF16)


def _rope_table():
    rows = SEQ // GRID_W
    row = jnp.repeat(jnp.arange(rows, dtype=F32), GRID_W)
    col = jnp.tile(jnp.arange(GRID_W, dtype=F32), rows)

    def cos_sin(rot_dim):
        n_freq = rot_dim // 4
        inv = ROPE_THETA ** (-jnp.arange(n_freq, dtype=F32) / n_freq)
        ang = jnp.concatenate([row[:, None] * inv[None], col[:, None] * inv[None]], axis=-1)
        return jnp.cos(ang), jnp.sin(ang)

    cos, sin = cos_sin(HEAD_DIM)
    cos_m, sin_m = cos_sin(MLA_ROPE)
    z32 = jnp.zeros_like(sin_m)
    z64 = jnp.zeros_like(sin)
    lat = jnp.concatenate([
        cos, cos,
        -sin, sin,
        cos_m, cos_m, z64,
        -sin_m, z32, z64,
        z32, sin_m, z64,
    ], axis=-1)
    ctx_row = jnp.concatenate([jnp.ones((LANES,), F32), jnp.zeros((LANES,), F32),
                               jnp.ones((LANES,), F32), jnp.zeros((2 * LANES,), F32)])
    return jnp.concatenate([lat, jnp.broadcast_to(ctx_row, (CTX_LEN, TAB_WIDTH))], axis=0)


def kernel(x, c, ctx, c_ctx, w_ada, b_ada, norm_g, w_in, q_gain, k_gain, cq_gain, ckv_gain,
           w_uq, w_ukv, w_out, final_g):
    n_b = x.shape[0]
    assert x.shape == (n_b, SEQ, D_MODEL) and ctx.shape == (n_b, CTX_LEN, D_MODEL)
    assert n_b == ADA_CTX_ROW and w_ada.shape[0] == DEPTH and w_in.shape[-1] == D_IN

    c_all = jnp.concatenate(
        [c, c_ctx[None], jnp.zeros((ADA_ROWS - n_b - 1, D_MODEL), F32)], axis=0)
    mod = _ada_rows(c_all, w_ada, b_ada).reshape(DEPTH, ADA_ROWS, 3, D_MODEL)

    row3 = lambda a: a.reshape(DEPTH, 1, a.shape[-1])
    params = (row3(norm_g), jnp.swapaxes(w_in, 1, 2).astype(BF16), row3(q_gain), row3(k_gain), row3(cq_gain),
              row3(ckv_gain), _pack_w_uq(w_uq), w_ukv.astype(BF16))
    w_out_b = w_out.astype(BF16)
    tab = _rope_table()

    stream, ctx_blk = (x, ctx), 0
    for layer in range(DEPTH):
        last = layer == DEPTH - 1
        qa, ka, va, qm, km, vm, g = _in_proj(layer, stream, ctx_blk, mod, params, tab)
        y_lat = (_attn_gqa(qa, ka, va, g), _attn_mla(qm, km, vm, g))
        y_ctx = None if last else _attn_ctx(qa, ka, va, qm, km, vm, g)
        xs = _out_proj(layer, y_lat, y_ctx, w_out_b, stream, ctx_blk, mod, final_g)
        stream, ctx_blk = (xs, xs), SEQ // CTX_LEN
    return xs
```

```python
import functools

import jax
import jax.numpy as jnp
import numpy as np
from jax import lax
from jax.experimental import pallas as pl
from jax.experimental.pallas import tpu as pltpu

F32 = jnp.float32
BF16 = jnp.bfloat16

D_MODEL = 2048
SEQ = 2048
CTX_LEN = 256
T_TOK = SEQ + CTX_LEN
GRID_W = 64
ROPE_THETA = 10000.0
EPS = 1e-6
DEPTH = 4
LOG2_E = 1.4426950408889634

LANES = 128
HEAD_DIM = 128
GQA_HEADS = 8
GQA_KV_HEADS = 2
GQA_GROUP = GQA_HEADS // GQA_KV_HEADS
GQA_WIDTH = GQA_HEADS * HEAD_DIM
GQA_KV_WIDTH = GQA_KV_HEADS * HEAD_DIM

MLA_HEADS = 8
MLA_NOPE = 128
MLA_ROPE = 64
MLA_QK = MLA_NOPE + MLA_ROPE
MLA_V = 128
MLA_Q_RANK = 512
MLA_KV_RANK = 512
MLA_WIDTH = MLA_HEADS * MLA_V
D_MIX = GQA_WIDTH + MLA_WIDTH
MLA_DK = MLA_NOPE + LANES
MLA_QK_WIDTH = MLA_HEADS * MLA_DK
V_EXT = 2 * HEAD_DIM

COL_Q = 0
COL_K = COL_Q + GQA_WIDTH
COL_V = COL_K + GQA_KV_WIDTH
COL_GG = COL_V + GQA_KV_WIDTH
COL_CQ = COL_GG + GQA_WIDTH
COL_CKV = COL_CQ + MLA_Q_RANK
COL_KR = COL_CKV + MLA_KV_RANK
COL_GM = COL_KR + MLA_ROPE
D_IN = COL_GM + MLA_WIDTH

TAB_COS_A, TAB_SIN_A, TAB_COS_M, TAB_SIN_M_LO, TAB_SIN_M_HI = range(5)
TAB_WIDTH = 5 * LANES

ADA_ROWS = 8
ADA_CTX_ROW = 4
ADA_TN = 1024

TM = 256
TQ_GQA = 512
TQ_MLA = 1024
N_BLK = T_TOK // TM
N_LAT_BLK = SEQ // TM
TM_OUT = 512
N_LAT_OUT = SEQ // TM_OUT
MLA_HEADS_PER_STEP = 4

VMEM_LIMIT = 56 * 1024 * 1024

assert CTX_LEN == TM and SEQ % TQ_GQA == 0 and SEQ % TQ_MLA == 0


def _silu(x):
    return x / (1.0 + jnp.exp(-x))


def _rms(x):
    return x * lax.rsqrt(jnp.mean(x * x, axis=-1, keepdims=True) + EPS)


def _tab(tab_ref, col):
    return tab_ref[:, col * LANES:(col + 1) * LANES]


def _rot_a(x, tab_ref):
    return x * _tab(tab_ref, TAB_COS_A) + pltpu.roll(x, HEAD_DIM // 2, 1) * _tab(tab_ref, TAB_SIN_A)


def _rot_m(x, tab_ref, scale=None):
    half = MLA_ROPE // 2
    y = (x * _tab(tab_ref, TAB_COS_M)
         + pltpu.roll(x, LANES - half, 1) * _tab(tab_ref, TAB_SIN_M_LO)
         + pltpu.roll(x, half, 1) * _tab(tab_ref, TAB_SIN_M_HI))
    return y if scale is None else y * scale


def _mod_row(b, j):
    return jnp.where(j == N_LAT_BLK, ADA_CTX_ROW, b)


def _stream_specs(ctx_blk):
    return [pl.BlockSpec((None, TM, D_MODEL), lambda b, j: (b, jnp.minimum(j, N_LAT_BLK - 1), 0)),
            pl.BlockSpec((None, CTX_LEN, D_MODEL), lambda b, j: (b, ctx_blk, 0))]


def _stream_block(xl_ref, xc_ref):
    return jnp.where(pl.program_id(1) == N_LAT_BLK, xc_ref[...], xl_ref[...])


def _ada_kernel(c_ref, w_ref, b_ref, o_ref):
    s = _silu(c_ref[...]).astype(BF16)
    w = w_ref[...].astype(BF16)
    o_ref[...] = jnp.dot(s, w, preferred_element_type=F32) + b_ref[...]


def _ada_rows(c_all, w_ada, b_ada):
    n_out = 3 * D_MODEL
    return pl.pallas_call(
        _ada_kernel,
        grid=(DEPTH, n_out // ADA_TN),
        in_specs=[
            pl.BlockSpec((ADA_ROWS, D_MODEL), lambda l, n: (0, 0)),
            pl.BlockSpec((None, D_MODEL, ADA_TN), lambda l, n: (l, 0, n)),
            pl.BlockSpec((None, 1, ADA_TN), lambda l, n: (l, 0, n)),
        ],
        out_specs=pl.BlockSpec((None, ADA_ROWS, ADA_TN), lambda l, n: (l, 0, n)),
        out_shape=jax.ShapeDtypeStruct((DEPTH, ADA_ROWS, n_out), F32),
        compiler_params=pltpu.CompilerParams(
            dimension_semantics=("arbitrary", "arbitrary"), vmem_limit_bytes=VMEM_LIMIT),
        name="ada_rows",
    )(c_all, w_ada, b_ada.reshape(DEPTH, 1, n_out))


def _in_proj_kernel(xl_ref, xc_ref, mod_ref, ng_ref, w_in_ref, qg_ref, kg_ref, cqg_ref, ckvg_ref,
                    w_uq_ref, w_ukv_ref, tab_ref,
                    qa_ref, ka_ref, va_ref, qm_ref, km_ref, vm_ref, g_ref):
    x = _stream_block(xl_ref, xc_ref)
    h = _rms(x) * ng_ref[...] * (1.0 + mod_ref[1:2, :]) + mod_ref[0:1, :]
    hb = h.astype(BF16)
    proj = lambda r0, r1: lax.dot_general(hb, w_in_ref[r0:r1, :], (((1,), (1,)), ((), ())),
                                          preferred_element_type=F32)
    c = proj(COL_CQ, COL_KR)
    cq = (_rms(c[:, :MLA_Q_RANK]) * cqg_ref[...]).astype(BF16)
    ckv = (_rms(c[:, MLA_Q_RANK:]) * ckvg_ref[...]).astype(BF16)

    p = proj(COL_Q, COL_GG)
    q_gain = qg_ref[...] * (HEAD_DIM ** -0.5 * LOG2_E)
    for i in range(GQA_HEADS):
        c0 = COL_Q + i * HEAD_DIM
        qh = _rms(p[:, c0:c0 + HEAD_DIM]) * q_gain
        qa_ref[:, i * HEAD_DIM:(i + 1) * HEAD_DIM] = _rot_a(qh, tab_ref).astype(BF16)
    k_gain = kg_ref[...]
    for i in range(GQA_KV_HEADS):
        c0 = COL_K + i * HEAD_DIM
        kh = _rms(p[:, c0:c0 + HEAD_DIM]) * k_gain
        ka_ref[:, i * HEAD_DIM:(i + 1) * HEAD_DIM] = _rot_a(kh, tab_ref).astype(BF16)
    ones = jnp.ones((TM, HEAD_DIM), BF16)
    for i in range(GQA_KV_HEADS):
        c0 = COL_V + i * HEAD_DIM
        va_ref[:, i * V_EXT:i * V_EXT + HEAD_DIM] = p[:, c0:c0 + HEAD_DIM].astype(BF16)
        va_ref[:, i * V_EXT + HEAD_DIM:(i + 1) * V_EXT] = ones

    g_ref[:, :GQA_WIDTH] = _silu(proj(COL_GG, COL_CQ)).astype(BF16)
    g_ref[:, GQA_WIDTH:] = _silu(proj(COL_GM, D_IN)).astype(BF16)
    kr_in = proj(COL_KR, COL_KR + LANES)
    kr_in = jnp.where(lax.broadcasted_iota(jnp.int32, kr_in.shape, 1) < MLA_ROPE, kr_in, 0.0)

    qm = jnp.dot(cq, w_uq_ref[...], preferred_element_type=F32)
    m_scale = MLA_QK ** -0.5 * LOG2_E
    for i in range(MLA_HEADS):
        c0 = i * MLA_DK
        qm_ref[:, c0:c0 + MLA_NOPE] = (qm[:, c0:c0 + MLA_NOPE] * m_scale).astype(BF16)
        qm_ref[:, c0 + MLA_NOPE:c0 + MLA_DK] = _rot_m(
            qm[:, c0 + MLA_NOPE:c0 + MLA_DK], tab_ref, m_scale).astype(BF16)

    kvm = jnp.dot(ckv, w_ukv_ref[...], preferred_element_type=F32)
    kr = _rot_m(kr_in, tab_ref).astype(BF16)
    for i in range(MLA_HEADS):
        c0 = i * (MLA_NOPE + MLA_V)
        km_ref[:, i * MLA_DK:i * MLA_DK + MLA_NOPE] = kvm[:, c0:c0 + MLA_NOPE].astype(BF16)
        km_ref[:, i * MLA_DK + MLA_NOPE:(i + 1) * MLA_DK] = kr
        vm_ref[:, i * V_EXT:i * V_EXT + MLA_V] = kvm[:, c0 + MLA_NOPE:c0 + MLA_NOPE + MLA_V].astype(BF16)
        vm_ref[:, i * V_EXT + MLA_V:(i + 1) * V_EXT] = ones


def _in_proj(layer, stream, ctx_blk, mod, params, tab):
    n_b = stream[0].shape[0]
    tok = lambda width: pl.BlockSpec((None, TM, width), lambda b, j: (b, j, 0))
    lay = lambda *shape: pl.BlockSpec((None,) + shape, lambda b, j: (layer,) + (0,) * len(shape),
                                      pipeline_mode=pl.Buffered(1))
    widths = (GQA_WIDTH, GQA_KV_WIDTH, GQA_KV_HEADS * V_EXT,
              MLA_QK_WIDTH, MLA_QK_WIDTH, MLA_HEADS * V_EXT, D_MIX)
    return pl.pallas_call(
        _in_proj_kernel,
        grid=(n_b, N_BLK),
        in_specs=_stream_specs(ctx_blk) + [
            pl.BlockSpec((None, None, 3, D_MODEL), lambda b, j: (layer, _mod_row(b, j), 0, 0)),
            lay(1, D_MODEL),
            lay(D_IN, D_MODEL),
            lay(1, HEAD_DIM), lay(1, HEAD_DIM), lay(1, MLA_Q_RANK), lay(1, MLA_KV_RANK),
            lay(MLA_Q_RANK, MLA_QK_WIDTH),
            lay(MLA_KV_RANK, MLA_HEADS * (MLA_NOPE + MLA_V)),
            pl.BlockSpec((TM, TAB_WIDTH), lambda b, j: (j, 0)),
        ],
        out_specs=[tok(w) for w in widths],
        out_shape=[jax.ShapeDtypeStruct((n_b, T_TOK, w), BF16) for w in widths],
        compiler_params=pltpu.CompilerParams(
            dimension_semantics=("arbitrary", "arbitrary"), vmem_limit_bytes=VMEM_LIMIT),
        name="in_proj",
    )(*stream, mod, *params, tab)


def _attend(q_ref, k_ref, v_ref, g_ref, o_ref, *, kv_heads, group, dk, dv):
    for p in range(kv_heads):
        k = k_ref[:, p * dk:(p + 1) * dk]
        v = v_ref[:, p * V_EXT:(p + 1) * V_EXT]
        for r0 in range(0, q_ref.shape[0], TM):
            for i in range(group):
                hd = p * group + i
                q = q_ref[r0:r0 + TM, hd * dk:(hd + 1) * dk]
                s = lax.dot_general(q, k, (((1,), (1,)), ((), ())), preferred_element_type=F32)
                e = jnp.exp2(s - jnp.max(s, axis=-1, keepdims=True))
                o = jnp.dot(e.astype(BF16), v, preferred_element_type=F32)
                gate = g_ref[r0:r0 + TM, hd * dv:(hd + 1) * dv].astype(F32)
                o_ref[r0:r0 + TM, hd * dv:(hd + 1) * dv] = (
                    o[:, :dv] * (gate / o[:, dv:])).astype(BF16)


_GQA_CFG = dict(kv_heads=GQA_KV_HEADS, group=GQA_GROUP, dk=HEAD_DIM, dv=HEAD_DIM)


def _attn_gqa(q, k, v, g):
    n_b = q.shape[0]
    return pl.pallas_call(
        functools.partial(_attend, **_GQA_CFG),
        grid=(n_b, SEQ // TQ_GQA),
        in_specs=[
            pl.BlockSpec((None, TQ_GQA, GQA_WIDTH), lambda b, j: (b, j, 0)),
            pl.BlockSpec((None, T_TOK, GQA_KV_WIDTH), lambda b, j: (b, 0, 0)),
            pl.BlockSpec((None, T_TOK, GQA_KV_HEADS * V_EXT), lambda b, j: (b, 0, 0)),
            pl.BlockSpec((None, TQ_GQA, GQA_WIDTH), lambda b, j: (b, j, 0)),
        ],
        out_specs=pl.BlockSpec((None, TQ_GQA, GQA_WIDTH), lambda b, j: (b, j, 0)),
        out_shape=jax.ShapeDtypeStruct((n_b, SEQ, GQA_WIDTH), BF16),
        compiler_params=pltpu.CompilerParams(
            dimension_semantics=("arbitrary", "arbitrary"), vmem_limit_bytes=VMEM_LIMIT),
        name="attn_gqa",
    )(q, k, v, g)


def _attn_mla(q, k, v, g):
    n_b = q.shape[0]
    hps = MLA_HEADS_PER_STEP
    g_col0 = GQA_WIDTH // (hps * MLA_V)
    return pl.pallas_call(
        functools.partial(_attend, kv_heads=hps, group=1, dk=MLA_DK, dv=MLA_V),
        grid=(n_b, MLA_HEADS // hps, SEQ // TQ_MLA),
        in_specs=[
            pl.BlockSpec((None, TQ_MLA, hps * MLA_DK), lambda b, hg, j: (b, j, hg)),
            pl.BlockSpec((None, T_TOK, hps * MLA_DK), lambda b, hg, j: (b, 0, hg)),
            pl.BlockSpec((None, T_TOK, hps * V_EXT), lambda b, hg, j: (b, 0, hg)),
            pl.BlockSpec((None, TQ_MLA, hps * MLA_V), lambda b, hg, j: (b, j, g_col0 + hg)),
        ],
        out_specs=pl.BlockSpec((None, TQ_MLA, hps * MLA_V), lambda b, hg, j: (b, j, hg)),
        out_shape=jax.ShapeDtypeStruct((n_b, SEQ, MLA_WIDTH), BF16),
        compiler_params=pltpu.CompilerParams(
            dimension_semantics=("arbitrary", "arbitrary", "arbitrary"), vmem_limit_bytes=VMEM_LIMIT),
        name="attn_mla",
    )(q, k, v, g)


def _attn_ctx_kernel(qa_ref, ka_ref, va_ref, qm_ref, km_ref, vm_ref, g_ref, oa_ref, om_ref):
    _attend(qa_ref, ka_ref, va_ref, g_ref.at[:, :GQA_WIDTH], oa_ref, **_GQA_CFG)
    _attend(qm_ref, km_ref, vm_ref, g_ref.at[:, GQA_WIDTH:], om_ref,
            kv_heads=MLA_HEADS, group=1, dk=MLA_DK, dv=MLA_V)


def _attn_ctx(qa, ka, va, qm, km, vm, g):
    n_b = qa.shape[0]
    ctx_rows = lambda width: pl.BlockSpec((None, CTX_LEN, width), lambda b: (b, SEQ // CTX_LEN, 0))
    out = lambda width: pl.BlockSpec((None, CTX_LEN, width), lambda b: (b, 0, 0))
    return pl.pallas_call(
        _attn_ctx_kernel,
        grid=(n_b,),
        in_specs=[ctx_rows(GQA_WIDTH), ctx_rows(GQA_KV_WIDTH), ctx_rows(GQA_KV_HEADS * V_EXT),
                  ctx_rows(MLA_QK_WIDTH), ctx_rows(MLA_QK_WIDTH), ctx_rows(MLA_HEADS * V_EXT),
                  ctx_rows(D_MIX)],
        out_specs=[out(GQA_WIDTH), out(MLA_WIDTH)],
        out_shape=[jax.ShapeDtypeStruct((n_b, CTX_LEN, GQA_WIDTH), BF16),
                   jax.ShapeDtypeStruct((n_b, CTX_LEN, MLA_WIDTH), BF16)],
        compiler_params=pltpu.CompilerParams(
            dimension_semantics=("arbitrary",), vmem_limit_bytes=VMEM_LIMIT),
        name="attn_ctx",
    )(qa, ka, va, qm, km, vm, g)


def _out_proj_kernel(*refs, last):
    def residual(a_ref, m_ref, x_ref):
        y = jnp.concatenate([a_ref[...], m_ref[...]], axis=-1)
        return x_ref[...] + mod_ref[2:3, :] * jnp.dot(y, w_ref[...], preferred_element_type=F32)

    if last:
        ya_ref, ym_ref, w_ref, xl_ref, mod_ref, fg_ref, o_ref = refs
        o_ref[...] = _rms(residual(ya_ref, ym_ref, xl_ref)) * fg_ref[...]
        return
    ya_ref, ym_ref, yca_ref, ycm_ref, w_ref, xl_ref, xc_ref, mod_ref, o_ref = refs
    j = pl.program_id(1)

    @pl.when(j < N_LAT_OUT)
    def _():
        o_ref[...] = residual(ya_ref, ym_ref, xl_ref)

    @pl.when(j == N_LAT_OUT)
    def _():
        o_ref[:CTX_LEN, :] = residual(yca_ref, ycm_ref, xc_ref)
        o_ref[CTX_LEN:, :] = jnp.zeros((TM_OUT - CTX_LEN, D_MODEL), F32)


def _out_proj(layer, y_lat, y_ctx, w_out_b, stream, ctx_blk, mod, final_g):
    n_b = y_lat[0].shape[0]
    last = y_ctx is None
    n_blk = N_LAT_OUT if last else N_LAT_OUT + 1
    lat = lambda width: pl.BlockSpec((None, TM_OUT, width),
                                     lambda b, j: (b, jnp.minimum(j, N_LAT_OUT - 1), 0))
    ctx = lambda width, blk: pl.BlockSpec((None, CTX_LEN, width), lambda b, j: (b, blk, 0))
    w_spec = pl.BlockSpec((None, D_MIX, D_MODEL), lambda b, j: (layer, 0, 0),
                          pipeline_mode=pl.Buffered(1))
    mod_spec = pl.BlockSpec((None, None, 3, D_MODEL),
                            lambda b, j: (layer, jnp.where(j == N_LAT_OUT, ADA_CTX_ROW, b), 0, 0))
    if last:
        in_specs = [lat(GQA_WIDTH), lat(MLA_WIDTH), w_spec, lat(D_MODEL), mod_spec,
                    pl.BlockSpec((1, D_MODEL), lambda b, j: (0, 0))]
        args = (*y_lat, w_out_b, stream[0], mod, final_g.reshape(1, D_MODEL))
    else:
        in_specs = [lat(GQA_WIDTH), lat(MLA_WIDTH), ctx(GQA_WIDTH, 0), ctx(MLA_WIDTH, 0), w_spec,
                    lat(D_MODEL), ctx(D_MODEL, ctx_blk), mod_spec]
        args = (*y_lat, *y_ctx, w_out_b, *stream, mod)
    return pl.pallas_call(
        functools.partial(_out_proj_kernel, last=last),
        grid=(n_b, n_blk),
        in_specs=in_specs,
        out_specs=pl.BlockSpec((None, TM_OUT, D_MODEL), lambda b, j: (b, j, 0)),
        out_shape=jax.ShapeDtypeStruct((n_b, n_blk * TM_OUT, D_MODEL), F32),
        compiler_params=pltpu.CompilerParams(
            dimension_semantics=("arbitrary", "arbitrary"), vmem_limit_bytes=VMEM_LIMIT),
        name="out_proj",
    )(*args)


def _pack_w_uq(w_uq):
    w = w_uq.reshape(DEPTH, MLA_Q_RANK, MLA_HEADS, MLA_QK)
    w = jnp.pad(w, ((0, 0), (0, 0), (0, 0), (0, MLA_DK - MLA_QK)))
    return w.reshape(DEPTH, MLA_Q_RANK, MLA_QK_WIDTH).astype(BF16)


def _rope_table():
    f32 = np.float32
    rows = SEQ // GRID_W
    row = np.repeat(np.arange(rows, dtype=f32), GRID_W)
    col = np.tile(np.arange(GRID_W, dtype=f32), rows)

    def cos_sin(rot_dim):
        n_freq = rot_dim // 4
        inv = (f32(ROPE_THETA) ** (-np.arange(n_freq, dtype=f32) / f32(n_freq))).astype(f32)
        ang = np.concatenate([row[:, None] * inv[None], col[:, None] * inv[None]], axis=-1)
        return np.cos(ang).astype(f32), np.sin(ang).astype(f32)

    cos, sin = cos_sin(HEAD_DIM)
    cos_m, sin_m = cos_sin(MLA_ROPE)
    z32 = np.zeros_like(sin_m)
    z64 = np.zeros_like(sin)
    lat = np.concatenate([
        cos, cos,
        -sin, sin,
        cos_m, cos_m, z64,
        -sin_m, z32, z64,
        z32, sin_m, z64,
    ], axis=-1)
    ctx_row = np.concatenate([np.ones(LANES, f32), np.zeros(LANES, f32),
                              np.ones(LANES, f32), np.zeros(2 * LANES, f32)])
    return jnp.asarray(np.concatenate([lat, np.broadcast_to(ctx_row, (CTX_LEN, TAB_WIDTH))], axis=0))


def kernel(x, c, ctx, c_ctx, w_ada, b_ada, norm_g, w_in, q_gain, k_gain, cq_gain, ckv_gain,
           w_uq, w_ukv, w_out, final_g):
    n_b = x.shape[0]
    assert x.shape == (n_b, SEQ, D_MODEL) and ctx.shape == (n_b, CTX_LEN, D_MODEL)
    assert n_b == ADA_CTX_ROW and w_ada.shape[0] == DEPTH and w_in.shape[-1] == D_IN

    c_all = jnp.concatenate(
        [c, c_ctx[None], jnp.zeros((ADA_ROWS - n_b - 1, D_MODEL), F32)], axis=0)
    mod = _ada_rows(c_all, w_ada, b_ada).reshape(DEPTH, ADA_ROWS, 3, D_MODEL)

    row3 = lambda a: a.reshape(DEPTH, 1, a.shape[-1])
    params = (row3(norm_g), jnp.swapaxes(w_in, 1, 2).astype(BF16), row3(q_gain), row3(k_gain), row3(cq_gain),
              row3(ckv_gain), _pack_w_uq(w_uq), w_ukv.astype(BF16))
    w_out_b = w_out.astype(BF16)
    tab = _rope_table()

    stream, ctx_blk = (x, ctx), 0
    for layer in range(DEPTH):
        last = layer == DEPTH - 1
        qa, ka, va, qm, km, vm, g = _in_proj(layer, stream, ctx_blk, mod, params, tab)
        y_lat = (_attn_gqa(qa, ka, va, g), _attn_mla(qm, km, vm, g))
        y_ctx = None if last else _attn_ctx(qa, ka, va, qm, km, vm, g)
        xs = _out_proj(layer, y_lat, y_ctx, w_out_b, stream, ctx_blk, mod, final_g)
        stream, ctx_blk = (xs, xs), SEQ // CTX_LEN
    return xs
```

```python
import functools

import jax
import jax.numpy as jnp
import numpy as np
from jax import lax
from jax.experimental import pallas as pl
from jax.experimental.pallas import tpu as pltpu

F32 = jnp.float32
BF16 = jnp.bfloat16

D_MODEL = 2048
SEQ = 2048
CTX_LEN = 256
T_TOK = SEQ + CTX_LEN
GRID_W = 64
ROPE_THETA = 10000.0
EPS = 1e-6
DEPTH = 4
LOG2_E = 1.4426950408889634

LANES = 128
HEAD_DIM = 128
GQA_HEADS = 8
GQA_KV_HEADS = 2
GQA_GROUP = GQA_HEADS // GQA_KV_HEADS
GQA_WIDTH = GQA_HEADS * HEAD_DIM
GQA_KV_WIDTH = GQA_KV_HEADS * HEAD_DIM

MLA_HEADS = 8
MLA_NOPE = 128
MLA_ROPE = 64
MLA_QK = MLA_NOPE + MLA_ROPE
MLA_V = 128
MLA_Q_RANK = 512
MLA_KV_RANK = 512
MLA_WIDTH = MLA_HEADS * MLA_V
D_MIX = GQA_WIDTH + MLA_WIDTH
MLA_DK = MLA_NOPE + LANES
MLA_QK_WIDTH = MLA_HEADS * MLA_DK
V_EXT = 2 * HEAD_DIM

COL_Q = 0
COL_K = COL_Q + GQA_WIDTH
COL_V = COL_K + GQA_KV_WIDTH
COL_GG = COL_V + GQA_KV_WIDTH
COL_CQ = COL_GG + GQA_WIDTH
COL_CKV = COL_CQ + MLA_Q_RANK
COL_KR = COL_CKV + MLA_KV_RANK
COL_GM = COL_KR + MLA_ROPE
D_IN = COL_GM + MLA_WIDTH

TAB_COS_A, TAB_SIN_A, TAB_COS_M, TAB_SIN_M_LO, TAB_SIN_M_HI = range(5)
TAB_WIDTH = 5 * LANES

ADA_ROWS = 8
ADA_CTX_ROW = 4
ADA_TN = 1024

TM = 256
TQ_GQA = 512
TQ_MLA = 1024
N_BLK = T_TOK // TM
N_LAT_BLK = SEQ // TM
TM_OUT = 512
N_LAT_OUT = SEQ // TM_OUT
X_SLOTS = 3
MLA_HEADS_PER_STEP = 4

VMEM_LIMIT = 56 * 1024 * 1024

assert CTX_LEN == TM and SEQ % TQ_GQA == 0 and SEQ % TQ_MLA == 0


def _silu(x):
    return x / (1.0 + jnp.exp(-x))


def _rms(x):
    return x * lax.rsqrt(jnp.mean(x * x, axis=-1, keepdims=True) + EPS)


def _tab(tab_ref, col):
    return tab_ref[:, col * LANES:(col + 1) * LANES]


def _rot_a(x, tab_ref):
    return x * _tab(tab_ref, TAB_COS_A) + pltpu.roll(x, HEAD_DIM // 2, 1) * _tab(tab_ref, TAB_SIN_A)


def _rot_m(x, tab_ref, scale=None):
    half = MLA_ROPE // 2
    y = (x * _tab(tab_ref, TAB_COS_M)
         + pltpu.roll(x, LANES - half, 1) * _tab(tab_ref, TAB_SIN_M_LO)
         + pltpu.roll(x, half, 1) * _tab(tab_ref, TAB_SIN_M_HI))
    return y if scale is None else y * scale


def _mod_row(b, j):
    return jnp.where(j == N_LAT_BLK, ADA_CTX_ROW, b)


def _stream_specs(ctx_blk):
    return [pl.BlockSpec((None, TM, D_MODEL), lambda b, j: (b, jnp.minimum(j, N_LAT_BLK - 1), 0)),
            pl.BlockSpec((None, CTX_LEN, D_MODEL), lambda b, j: (b, ctx_blk, 0))]


def _stream_block(xl_ref, xc_ref):
    return jnp.where(pl.program_id(1) == N_LAT_BLK, xc_ref[...], xl_ref[...])


def _ada_kernel(c_ref, w_ref, b_ref, o_ref):
    s = _silu(c_ref[...]).astype(BF16)
    w = w_ref[...].astype(BF16)
    o_ref[...] = jnp.dot(s, w, preferred_element_type=F32) + b_ref[...]


def _ada_rows(c_all, w_ada, b_ada):
    n_out = 3 * D_MODEL
    return pl.pallas_call(
        _ada_kernel,
        grid=(DEPTH, n_out // ADA_TN),
        in_specs=[
            pl.BlockSpec((ADA_ROWS, D_MODEL), lambda l, n: (0, 0)),
            pl.BlockSpec((None, D_MODEL, ADA_TN), lambda l, n: (l, 0, n)),
            pl.BlockSpec((None, 1, ADA_TN), lambda l, n: (l, 0, n)),
        ],
        out_specs=pl.BlockSpec((None, ADA_ROWS, ADA_TN), lambda l, n: (l, 0, n)),
        out_shape=jax.ShapeDtypeStruct((DEPTH, ADA_ROWS, n_out), F32),
        compiler_params=pltpu.CompilerParams(
            dimension_semantics=("arbitrary", "arbitrary"), vmem_limit_bytes=VMEM_LIMIT),
        name="ada_rows",
    )(c_all, w_ada, b_ada.reshape(DEPTH, 1, n_out))


def _in_proj_kernel(xl_ref, xc_ref, mod_ref, ng_ref, w_in_ref, qg_ref, kg_ref, cqg_ref, ckvg_ref,
                    w_uq_ref, w_ukv_ref, tab_ref,
                    qa_ref, ka_ref, va_ref, qm_ref, km_ref, vm_ref, g_ref):
    x = _stream_block(xl_ref, xc_ref)
    h = _rms(x) * ng_ref[...] * (1.0 + mod_ref[1:2, :]) + mod_ref[0:1, :]
    hb = h.astype(BF16)
    proj = lambda r0, r1: lax.dot_general(hb, w_in_ref[r0:r1, :], (((1,), (1,)), ((), ())),
                                          preferred_element_type=F32)
    c = proj(COL_CQ, COL_KR)
    cq = (_rms(c[:, :MLA_Q_RANK]) * cqg_ref[...]).astype(BF16)
    ckv = (_rms(c[:, MLA_Q_RANK:]) * ckvg_ref[...]).astype(BF16)

    p = proj(COL_Q, COL_GG)
    q_gain = qg_ref[...] * (HEAD_DIM ** -0.5 * LOG2_E)
    for i in range(GQA_HEADS):
        c0 = COL_Q + i * HEAD_DIM
        qh = _rms(p[:, c0:c0 + HEAD_DIM]) * q_gain
        qa_ref[:, i * HEAD_DIM:(i + 1) * HEAD_DIM] = _rot_a(qh, tab_ref).astype(BF16)
    k_gain = kg_ref[...]
    for i in range(GQA_KV_HEADS):
        c0 = COL_K + i * HEAD_DIM
        kh = _rms(p[:, c0:c0 + HEAD_DIM]) * k_gain
        ka_ref[:, i * HEAD_DIM:(i + 1) * HEAD_DIM] = _rot_a(kh, tab_ref).astype(BF16)
    ones = jnp.ones((TM, HEAD_DIM), BF16)
    for i in range(GQA_KV_HEADS):
        c0 = COL_V + i * HEAD_DIM
        va_ref[:, i * V_EXT:i * V_EXT + HEAD_DIM] = p[:, c0:c0 + HEAD_DIM].astype(BF16)
        va_ref[:, i * V_EXT + HEAD_DIM:(i + 1) * V_EXT] = ones

    g_ref[:, :GQA_WIDTH] = _silu(proj(COL_GG, COL_CQ)).astype(BF16)
    g_ref[:, GQA_WIDTH:] = _silu(proj(COL_GM, D_IN)).astype(BF16)
    kr_in = proj(COL_KR, COL_KR + LANES)
    kr_in = jnp.where(lax.broadcasted_iota(jnp.int32, kr_in.shape, 1) < MLA_ROPE, kr_in, 0.0)

    qm = jnp.dot(cq, w_uq_ref[...], preferred_element_type=F32)
    m_scale = MLA_QK ** -0.5 * LOG2_E
    for i in range(MLA_HEADS):
        c0 = i * MLA_DK
        qm_ref[:, c0:c0 + MLA_NOPE] = (qm[:, c0:c0 + MLA_NOPE] * m_scale).astype(BF16)
        qm_ref[:, c0 + MLA_NOPE:c0 + MLA_DK] = _rot_m(
            qm[:, c0 + MLA_NOPE:c0 + MLA_DK], tab_ref, m_scale).astype(BF16)

    kvm = jnp.dot(ckv, w_ukv_ref[...], preferred_element_type=F32)
    kr = _rot_m(kr_in, tab_ref).astype(BF16)
    for i in range(MLA_HEADS):
        c0 = i * (MLA_NOPE + MLA_V)
        km_ref[:, i * MLA_DK:i * MLA_DK + MLA_NOPE] = kvm[:, c0:c0 + MLA_NOPE].astype(BF16)
        km_ref[:, i * MLA_DK + MLA_NOPE:(i + 1) * MLA_DK] = kr
        vm_ref[:, i * V_EXT:i * V_EXT + MLA_V] = kvm[:, c0 + MLA_NOPE:c0 + MLA_NOPE + MLA_V].astype(BF16)
        vm_ref[:, i * V_EXT + MLA_V:(i + 1) * V_EXT] = ones


def _in_proj(layer, stream, ctx_blk, mod, params, tab):
    n_b = stream[0].shape[0]
    tok = lambda width: pl.BlockSpec((None, TM, width), lambda b, j: (b, j, 0))
    lay = lambda *shape: pl.BlockSpec((None,) + shape, lambda b, j: (layer,) + (0,) * len(shape),
                                      pipeline_mode=pl.Buffered(1))
    widths = (GQA_WIDTH, GQA_KV_WIDTH, GQA_KV_HEADS * V_EXT,
              MLA_QK_WIDTH, MLA_QK_WIDTH, MLA_HEADS * V_EXT, D_MIX)
    return pl.pallas_call(
        _in_proj_kernel,
        grid=(n_b, N_BLK),
        in_specs=_stream_specs(ctx_blk) + [
            pl.BlockSpec((None, None, 3, D_MODEL), lambda b, j: (layer, _mod_row(b, j), 0, 0)),
            lay(1, D_MODEL),
            lay(D_IN, D_MODEL),
            lay(1, HEAD_DIM), lay(1, HEAD_DIM), lay(1, MLA_Q_RANK), lay(1, MLA_KV_RANK),
            lay(MLA_Q_RANK, MLA_QK_WIDTH),
            lay(MLA_KV_RANK, MLA_HEADS * (MLA_NOPE + MLA_V)),
            pl.BlockSpec((TM, TAB_WIDTH), lambda b, j: (j, 0)),
        ],
        out_specs=[tok(w) for w in widths],
        out_shape=[jax.ShapeDtypeStruct((n_b, T_TOK, w), BF16) for w in widths],
        compiler_params=pltpu.CompilerParams(
            dimension_semantics=("arbitrary", "arbitrary"), vmem_limit_bytes=VMEM_LIMIT),
        name="in_proj",
    )(*stream, mod, *params, tab)


def _attend(q_ref, k_ref, v_ref, g_ref, o_ref, *, kv_heads, group, dk, dv):
    for p in range(kv_heads):
        k = k_ref[:, p * dk:(p + 1) * dk]
        v = v_ref[:, p * V_EXT:(p + 1) * V_EXT]
        for r0 in range(0, q_ref.shape[0], TM):
            for i in range(group):
                hd = p * group + i
                q = q_ref[r0:r0 + TM, hd * dk:(hd + 1) * dk]
                s = lax.dot_general(q, k, (((1,), (1,)), ((), ())), preferred_element_type=F32)
                e = jnp.exp2(s - jnp.max(s, axis=-1, keepdims=True))
                o = jnp.dot(e.astype(BF16), v, preferred_element_type=F32)
                gate = g_ref[r0:r0 + TM, hd * dv:(hd + 1) * dv].astype(F32)
                o_ref[r0:r0 + TM, hd * dv:(hd + 1) * dv] = (
                    o[:, :dv] * (gate / o[:, dv:])).astype(BF16)


_GQA_CFG = dict(kv_heads=GQA_KV_HEADS, group=GQA_GROUP, dk=HEAD_DIM, dv=HEAD_DIM)


def _attn_gqa(q, k, v, g):
    n_b = q.shape[0]
    return pl.pallas_call(
        functools.partial(_attend, **_GQA_CFG),
        grid=(n_b, SEQ // TQ_GQA),
        in_specs=[
            pl.BlockSpec((None, TQ_GQA, GQA_WIDTH), lambda b, j: (b, j, 0)),
            pl.BlockSpec((None, T_TOK, GQA_KV_WIDTH), lambda b, j: (b, 0, 0)),
            pl.BlockSpec((None, T_TOK, GQA_KV_HEADS * V_EXT), lambda b, j: (b, 0, 0)),
            pl.BlockSpec((None, TQ_GQA, GQA_WIDTH), lambda b, j: (b, j, 0)),
        ],
        out_specs=pl.BlockSpec((None, TQ_GQA, GQA_WIDTH), lambda b, j: (b, j, 0)),
        out_shape=jax.ShapeDtypeStruct((n_b, SEQ, GQA_WIDTH), BF16),
        compiler_params=pltpu.CompilerParams(
            dimension_semantics=("arbitrary", "arbitrary"), vmem_limit_bytes=VMEM_LIMIT),
        name="attn_gqa",
    )(q, k, v, g)


def _attn_mla(q, k, v, g):
    n_b = q.shape[0]
    hps = MLA_HEADS_PER_STEP
    g_col0 = GQA_WIDTH // (hps * MLA_V)
    return pl.pallas_call(
        functools.partial(_attend, kv_heads=hps, group=1, dk=MLA_DK, dv=MLA_V),
        grid=(n_b, MLA_HEADS // hps, SEQ // TQ_MLA),
        in_specs=[
            pl.BlockSpec((None, TQ_MLA, hps * MLA_DK), lambda b, hg, j: (b, j, hg)),
            pl.BlockSpec((None, T_TOK, hps * MLA_DK), lambda b, hg, j: (b, 0, hg)),
            pl.BlockSpec((None, T_TOK, hps * V_EXT), lambda b, hg, j: (b, 0, hg)),
            pl.BlockSpec((None, TQ_MLA, hps * MLA_V), lambda b, hg, j: (b, j, g_col0 + hg)),
        ],
        out_specs=pl.BlockSpec((None, TQ_MLA, hps * MLA_V), lambda b, hg, j: (b, j, hg)),
        out_shape=jax.ShapeDtypeStruct((n_b, SEQ, MLA_WIDTH), BF16),
        compiler_params=pltpu.CompilerParams(
            dimension_semantics=("arbitrary", "arbitrary", "arbitrary"), vmem_limit_bytes=VMEM_LIMIT),
        name="attn_mla",
    )(q, k, v, g)


def _attn_ctx_kernel(qa_ref, ka_ref, va_ref, qm_ref, km_ref, vm_ref, g_ref, oa_ref, om_ref):
    _attend(qa_ref, ka_ref, va_ref, g_ref.at[:, :GQA_WIDTH], oa_ref, **_GQA_CFG)
    _attend(qm_ref, km_ref, vm_ref, g_ref.at[:, GQA_WIDTH:], om_ref,
            kv_heads=MLA_HEADS, group=1, dk=MLA_DK, dv=MLA_V)


def _attn_ctx(qa, ka, va, qm, km, vm, g):
    n_b = qa.shape[0]
    ctx_rows = lambda width: pl.BlockSpec((None, CTX_LEN, width), lambda b: (b, SEQ // CTX_LEN, 0))
    out = lambda width: pl.BlockSpec((None, CTX_LEN, width), lambda b: (b, 0, 0))
    return pl.pallas_call(
        _attn_ctx_kernel,
        grid=(n_b,),
        in_specs=[ctx_rows(GQA_WIDTH), ctx_rows(GQA_KV_WIDTH), ctx_rows(GQA_KV_HEADS * V_EXT),
                  ctx_rows(MLA_QK_WIDTH), ctx_rows(MLA_QK_WIDTH), ctx_rows(MLA_HEADS * V_EXT),
                  ctx_rows(D_MIX)],
        out_specs=[out(GQA_WIDTH), out(MLA_WIDTH)],
        out_shape=[jax.ShapeDtypeStruct((n_b, CTX_LEN, GQA_WIDTH), BF16),
                   jax.ShapeDtypeStruct((n_b, CTX_LEN, MLA_WIDTH), BF16)],
        compiler_params=pltpu.CompilerParams(
            dimension_semantics=("arbitrary",), vmem_limit_bytes=VMEM_LIMIT),
        name="attn_ctx",
    )(qa, ka, va, qm, km, vm, g)


def _out_proj_kernel(*refs, last, n_blk):
    if last:
        ya_ref, ym_ref, w_ref, x_hbm, mod_ref, fg_ref, o_ref, xbuf, sem = refs
    else:
        ya_ref, ym_ref, yca_ref, ycm_ref, w_ref, x_hbm, xc_ref, mod_ref, o_ref, xbuf, sem = refs

    def residual(a_ref, m_ref, x_ref):
        y = jnp.concatenate([a_ref[...], m_ref[...]], axis=-1)
        return x_ref[...] + mod_ref[2:3, :] * jnp.dot(y, w_ref[...], preferred_element_type=F32)

    j = pl.program_id(1)
    step = pl.program_id(0) * n_blk + j
    n_steps = pl.num_programs(0) * n_blk

    def fetch(s):
        s = jnp.asarray(s, jnp.int32)
        row0 = pl.multiple_of((s % n_blk) * TM_OUT, TM_OUT)
        slot = s % X_SLOTS
        return pltpu.make_async_copy(x_hbm.at[s // n_blk, pl.ds(row0, TM_OUT), :],
                                     xbuf.at[slot], sem.at[slot])

    def start_fetch(s):
        s = jnp.asarray(s, jnp.int32)
        pl.when(jnp.logical_and(s < n_steps, s % n_blk < N_LAT_OUT))(lambda: fetch(s).start())

    @pl.when(step == 0)
    def _():
        for s in range(X_SLOTS - 1):
            start_fetch(s)

    start_fetch(step + (X_SLOTS - 1))

    @pl.when(j < N_LAT_OUT)
    def _():
        fetch(step).wait()
        x_new = residual(ya_ref, ym_ref, xbuf.at[step % X_SLOTS])
        o_ref[...] = _rms(x_new) * fg_ref[...] if last else x_new

    if not last:
        @pl.when(j == N_LAT_OUT)
        def _():
            o_ref[:CTX_LEN, :] = residual(yca_ref, ycm_ref, xc_ref)
            o_ref[CTX_LEN:, :] = jnp.zeros((TM_OUT - CTX_LEN, D_MODEL), F32)


def _out_proj(layer, y_lat, y_ctx, w_out_b, stream, ctx_blk, mod, final_g):
    n_b = y_lat[0].shape[0]
    last = y_ctx is None
    n_blk = N_LAT_OUT if last else N_LAT_OUT + 1
    lat = lambda width: pl.BlockSpec((None, TM_OUT, width),
                                     lambda b, j: (b, jnp.minimum(j, N_LAT_OUT - 1), 0))
    ctx = lambda width, blk: pl.BlockSpec((None, CTX_LEN, width), lambda b, j: (b, blk, 0))
    w_spec = pl.BlockSpec((None, D_MIX, D_MODEL), lambda b, j: (layer, 0, 0),
                          pipeline_mode=pl.Buffered(1))
    mod_spec = pl.BlockSpec((None, None, 3, D_MODEL),
                            lambda b, j: (layer, jnp.where(j == N_LAT_OUT, ADA_CTX_ROW, b), 0, 0))
    x_hbm = pl.BlockSpec(memory_space=pl.ANY)
    if last:
        in_specs = [lat(GQA_WIDTH), lat(MLA_WIDTH), w_spec, x_hbm, mod_spec,
                    pl.BlockSpec((1, D_MODEL), lambda b, j: (0, 0))]
        args = (*y_lat, w_out_b, stream[0], mod, final_g.reshape(1, D_MODEL))
    else:
        in_specs = [lat(GQA_WIDTH), lat(MLA_WIDTH), ctx(GQA_WIDTH, 0), ctx(MLA_WIDTH, 0), w_spec,
                    x_hbm, ctx(D_MODEL, ctx_blk), mod_spec]
        args = (*y_lat, *y_ctx, w_out_b, *stream, mod)
    return pl.pallas_call(
        functools.partial(_out_proj_kernel, last=last, n_blk=n_blk),
        grid=(n_b, n_blk),
        in_specs=in_specs,
        out_specs=pl.BlockSpec((None, TM_OUT, D_MODEL), lambda b, j: (b, j, 0)),
        out_shape=jax.ShapeDtypeStruct((n_b, n_blk * TM_OUT, D_MODEL), F32),
        scratch_shapes=[pltpu.VMEM((X_SLOTS, TM_OUT, D_MODEL), F32),
                        pltpu.SemaphoreType.DMA((X_SLOTS,))],
        compiler_params=pltpu.CompilerParams(
            dimension_semantics=("arbitrary", "arbitrary"), vmem_limit_bytes=VMEM_LIMIT),
        name="out_proj",
    )(*args)


def _pack_w_uq(w_uq):
    w = w_uq.reshape(DEPTH, MLA_Q_RANK, MLA_HEADS, MLA_QK)
    w = jnp.pad(w, ((0, 0), (0, 0), (0, 0), (0, MLA_DK - MLA_QK)))
    return w.reshape(DEPTH, MLA_Q_RANK, MLA_QK_WIDTH).astype(BF16)


def _rope_table():
    f32 = np.float32
    rows = SEQ // GRID_W
    row = np.repeat(np.arange(rows, dtype=f32), GRID_W)
    col = np.tile(np.arange(GRID_W, dtype=f32), rows)

    def cos_sin(rot_dim):
        n_freq = rot_dim // 4
        inv = (f32(ROPE_THETA) ** (-np.arange(n_freq, dtype=f32) / f32(n_freq))).astype(f32)
        ang = np.concatenate([row[:, None] * inv[None], col[:, None] * inv[None]], axis=-1)
        return np.cos(ang).astype(f32), np.sin(ang).astype(f32)

    cos, sin = cos_sin(HEAD_DIM)
    cos_m, sin_m = cos_sin(MLA_ROPE)
    z32 = np.zeros_like(sin_m)
    z64 = np.zeros_like(sin)
    lat = np.concatenate([
        cos, cos,
        -sin, sin,
        cos_m, cos_m, z64,
        -sin_m, z32, z64,
        z32, sin_m, z64,
    ], axis=-1)
    ctx_row = np.concatenate([np.ones(LANES, f32), np.zeros(LANES, f32),
                              np.ones(LANES, f32), np.zeros(2 * LANES, f32)])
    return jnp.asarray(np.concatenate([lat, np.broadcast_to(ctx_row, (CTX_LEN, TAB_WIDTH))], axis=0))


def kernel(x, c, ctx, c_ctx, w_ada, b_ada, norm_g, w_in, q_gain, k_gain, cq_gain, ckv_gain,
           w_uq, w_ukv, w_out, final_g):
    n_b = x.shape[0]
    assert x.shape == (n_b, SEQ, D_MODEL) and ctx.shape == (n_b, CTX_LEN, D_MODEL)
    assert n_b == ADA_CTX_ROW and w_ada.shape[0] == DEPTH and w_in.shape[-1] == D_IN

    c_all = jnp.concatenate(
        [c, c_ctx[None], jnp.zeros((ADA_ROWS - n_b - 1, D_MODEL), F32)], axis=0)
    mod = _ada_rows(c_all, w_ada, b_ada).reshape(DEPTH, ADA_ROWS, 3, D_MODEL)

    row3 = lambda a: a.reshape(DEPTH, 1, a.shape[-1])
    params = (row3(norm_g), jnp.swapaxes(w_in, 1, 2).astype(BF16), row3(q_gain), row3(k_gain), row3(cq_gain),
              row3(ckv_gain), _pack_w_uq(w_uq), w_ukv.astype(BF16))
    w_out_b = w_out.astype(BF16)
    tab = _rope_table()

    stream, ctx_blk = (x, ctx), 0
    for layer in range(DEPTH):
        last = layer == DEPTH - 1
        qa, ka, va, qm, km, vm, g = _in_proj(layer, stream, ctx_blk, mod, params, tab)
        y_lat = (_attn_gqa(qa, ka, va, g), _attn_mla(qm, km, vm, g))
        y_ctx = None if last else _attn_ctx(qa, ka, va, qm, km, vm, g)
        xs = _out_proj(layer, y_lat, y_ctx, w_out_b, stream, ctx_blk, mod, final_g)
        stream, ctx_blk = (xs, xs), SEQ // CTX_LEN
    return xs
```

```python
import functools

import jax
import jax.numpy as jnp
import numpy as np
from jax import lax
from jax.experimental import pallas as pl
from jax.experimental.pallas import tpu as pltpu

F32 = jnp.float32
BF16 = jnp.bfloat16

D_MODEL = 2048
SEQ = 2048
CTX_LEN = 256
T_TOK = SEQ + CTX_LEN
GRID_W = 64
ROPE_THETA = 10000.0
EPS = 1e-6
DEPTH = 4
LOG2_E = 1.4426950408889634

LANES = 128
HEAD_DIM = 128
GQA_HEADS = 8
GQA_KV_HEADS = 2
GQA_GROUP = GQA_HEADS // GQA_KV_HEADS
GQA_WIDTH = GQA_HEADS * HEAD_DIM
GQA_KV_WIDTH = GQA_KV_HEADS * HEAD_DIM

MLA_HEADS = 8
MLA_NOPE = 128
MLA_ROPE = 64
MLA_QK = MLA_NOPE + MLA_ROPE
MLA_V = 128
MLA_Q_RANK = 512
MLA_KV_RANK = 512
MLA_WIDTH = MLA_HEADS * MLA_V
D_MIX = GQA_WIDTH + MLA_WIDTH
MLA_DK = MLA_NOPE + LANES
MLA_QK_WIDTH = MLA_HEADS * MLA_DK
V_EXT = 2 * HEAD_DIM

COL_Q = 0
COL_K = COL_Q + GQA_WIDTH
COL_V = COL_K + GQA_KV_WIDTH
COL_GG = COL_V + GQA_KV_WIDTH
COL_CQ = COL_GG + GQA_WIDTH
COL_CKV = COL_CQ + MLA_Q_RANK
COL_KR = COL_CKV + MLA_KV_RANK
COL_GM = COL_KR + MLA_ROPE
D_IN = COL_GM + MLA_WIDTH

TAB_COS_A, TAB_SIN_A, TAB_COS_M, TAB_SIN_M_LO, TAB_SIN_M_HI = range(5)
TAB_WIDTH = 5 * LANES

ADA_ROWS = 8
ADA_CTX_ROW = 4
ADA_TN = 1024

TM = 256
TQ_GQA = 512
TQ_MLA = 1024
N_BLK = T_TOK // TM
N_LAT_BLK = SEQ // TM
TM_OUT = 512
N_LAT_OUT = SEQ // TM_OUT
X_SLOTS = 3
W_CHUNK = 64
W_SLOTS = 8
MLA_HEADS_PER_STEP = 4

VMEM_LIMIT = 56 * 1024 * 1024

assert CTX_LEN == TM and SEQ % TQ_GQA == 0 and SEQ % TQ_MLA == 0


def _silu(x):
    return x / (1.0 + jnp.exp(-x))


def _rms(x):
    return x * lax.rsqrt(jnp.mean(x * x, axis=-1, keepdims=True) + EPS)


def _tab(tab_ref, col):
    return tab_ref[:, col * LANES:(col + 1) * LANES]


def _rot_a(x, tab_ref):
    return x * _tab(tab_ref, TAB_COS_A) + pltpu.roll(x, HEAD_DIM // 2, 1) * _tab(tab_ref, TAB_SIN_A)


def _rot_m(x, tab_ref, scale=None):
    half = MLA_ROPE // 2
    y = (x * _tab(tab_ref, TAB_COS_M)
         + pltpu.roll(x, LANES - half, 1) * _tab(tab_ref, TAB_SIN_M_LO)
         + pltpu.roll(x, half, 1) * _tab(tab_ref, TAB_SIN_M_HI))
    return y if scale is None else y * scale


def _mod_row(b, j):
    return jnp.where(j == N_LAT_BLK, ADA_CTX_ROW, b)


def _stream_specs(ctx_blk):
    return [pl.BlockSpec((None, TM, D_MODEL), lambda b, j: (b, jnp.minimum(j, N_LAT_BLK - 1), 0)),
            pl.BlockSpec((None, CTX_LEN, D_MODEL), lambda b, j: (b, ctx_blk, 0))]


def _stream_block(xl_ref, xc_ref):
    return jnp.where(pl.program_id(1) == N_LAT_BLK, xc_ref[...], xl_ref[...])


def _ada_kernel(c_ref, w_ref, b_ref, o_ref):
    s = _silu(c_ref[...]).astype(BF16)
    w = w_ref[...].astype(BF16)
    o_ref[...] = jnp.dot(s, w, preferred_element_type=F32) + b_ref[...]


def _ada_rows(c_all, w_ada, b_ada):
    n_out = 3 * D_MODEL
    return pl.pallas_call(
        _ada_kernel,
        grid=(DEPTH, n_out // ADA_TN),
        in_specs=[
            pl.BlockSpec((ADA_ROWS, D_MODEL), lambda l, n: (0, 0)),
            pl.BlockSpec((None, D_MODEL, ADA_TN), lambda l, n: (l, 0, n)),
            pl.BlockSpec((None, 1, ADA_TN), lambda l, n: (l, 0, n)),
        ],
        out_specs=pl.BlockSpec((None, ADA_ROWS, ADA_TN), lambda l, n: (l, 0, n)),
        out_shape=jax.ShapeDtypeStruct((DEPTH, ADA_ROWS, n_out), F32),
        compiler_params=pltpu.CompilerParams(
            dimension_semantics=("arbitrary", "arbitrary"), vmem_limit_bytes=VMEM_LIMIT),
        name="ada_rows",
    )(c_all, w_ada, b_ada.reshape(DEPTH, 1, n_out))


def _load_w_in(w_hbm, layer, w_in_ref, stage, sem):
    n_chunks = D_IN // W_CHUNK

    def rows(c):
        row0 = c * W_CHUNK
        return pl.ds(row0 if isinstance(c, int) else pl.multiple_of(row0, W_CHUNK), W_CHUNK)

    def chunk(c):
        slot = c % W_SLOTS
        return pltpu.make_async_copy(w_hbm.at[layer, rows(c), :], stage.at[slot], sem.at[slot])

    for c in range(W_SLOTS):
        chunk(c).start()

    def convert(c, carry):
        chunk(c).wait()
        w_in_ref[rows(c), :] = stage[c % W_SLOTS].astype(BF16)
        pl.when(c + W_SLOTS < n_chunks)(lambda: chunk(c + W_SLOTS).start())
        return carry

    lax.fori_loop(0, n_chunks, convert, 0)


def _in_proj_kernel(xl_ref, xc_ref, mod_ref, ng_ref, w_hbm, qg_ref, kg_ref, cqg_ref, ckvg_ref,
                    w_uq_ref, w_ukv_ref, tab_ref,
                    qa_ref, ka_ref, va_ref, qm_ref, km_ref, vm_ref, g_ref,
                    w_in_ref, stage, sem, *, layer):
    @pl.when(jnp.logical_and(pl.program_id(0) == 0, pl.program_id(1) == 0))
    def _():
        _load_w_in(w_hbm, layer, w_in_ref, stage, sem)

    x = _stream_block(xl_ref, xc_ref)
    h = _rms(x) * ng_ref[...] * (1.0 + mod_ref[1:2, :]) + mod_ref[0:1, :]
    hb = h.astype(BF16)
    proj = lambda r0, r1: lax.dot_general(hb, w_in_ref[r0:r1, :], (((1,), (1,)), ((), ())),
                                          preferred_element_type=F32)
    c = proj(COL_CQ, COL_KR)
    cq = (_rms(c[:, :MLA_Q_RANK]) * cqg_ref[...]).astype(BF16)
    ckv = (_rms(c[:, MLA_Q_RANK:]) * ckvg_ref[...]).astype(BF16)

    p = proj(COL_Q, COL_GG)
    q_gain = qg_ref[...] * (HEAD_DIM ** -0.5 * LOG2_E)
    for i in range(GQA_HEADS):
        c0 = COL_Q + i * HEAD_DIM
        qh = _rms(p[:, c0:c0 + HEAD_DIM]) * q_gain
        qa_ref[:, i * HEAD_DIM:(i + 1) * HEAD_DIM] = _rot_a(qh, tab_ref).astype(BF16)
    k_gain = kg_ref[...]
    for i in range(GQA_KV_HEADS):
        c0 = COL_K + i * HEAD_DIM
        kh = _rms(p[:, c0:c0 + HEAD_DIM]) * k_gain
        ka_ref[:, i * HEAD_DIM:(i + 1) * HEAD_DIM] = _rot_a(kh, tab_ref).astype(BF16)
    ones = jnp.ones((TM, HEAD_DIM), BF16)
    for i in range(GQA_KV_HEADS):
        c0 = COL_V + i * HEAD_DIM
        va_ref[:, i * V_EXT:i * V_EXT + HEAD_DIM] = p[:, c0:c0 + HEAD_DIM].astype(BF16)
        va_ref[:, i * V_EXT + HEAD_DIM:(i + 1) * V_EXT] = ones

    g_ref[:, :GQA_WIDTH] = _silu(proj(COL_GG, COL_CQ)).astype(BF16)
    g_ref[:, GQA_WIDTH:] = _silu(proj(COL_GM, D_IN)).astype(BF16)
    kr_in = proj(COL_KR, COL_KR + LANES)
    kr_in = jnp.where(lax.broadcasted_iota(jnp.int32, kr_in.shape, 1) < MLA_ROPE, kr_in, 0.0)

    qm = jnp.dot(cq, w_uq_ref[...], preferred_element_type=F32)
    m_scale = MLA_QK ** -0.5 * LOG2_E
    for i in range(MLA_HEADS):
        c0 = i * MLA_DK
        qm_ref[:, c0:c0 + MLA_NOPE] = (qm[:, c0:c0 + MLA_NOPE] * m_scale).astype(BF16)
        qm_ref[:, c0 + MLA_NOPE:c0 + MLA_DK] = _rot_m(
            qm[:, c0 + MLA_NOPE:c0 + MLA_DK], tab_ref, m_scale).astype(BF16)

    kvm = jnp.dot(ckv, w_ukv_ref[...], preferred_element_type=F32)
    kr = _rot_m(kr_in, tab_ref).astype(BF16)
    for i in range(MLA_HEADS):
        c0 = i * (MLA_NOPE + MLA_V)
        km_ref[:, i * MLA_DK:i * MLA_DK + MLA_NOPE] = kvm[:, c0:c0 + MLA_NOPE].astype(BF16)
        km_ref[:, i * MLA_DK + MLA_NOPE:(i + 1) * MLA_DK] = kr
        vm_ref[:, i * V_EXT:i * V_EXT + MLA_V] = kvm[:, c0 + MLA_NOPE:c0 + MLA_NOPE + MLA_V].astype(BF16)
        vm_ref[:, i * V_EXT + MLA_V:(i + 1) * V_EXT] = ones


def _in_proj(layer, stream, ctx_blk, mod, params, tab):
    n_b = stream[0].shape[0]
    tok = lambda width: pl.BlockSpec((None, TM, width), lambda b, j: (b, j, 0))
    lay = lambda *shape: pl.BlockSpec((None,) + shape, lambda b, j: (layer,) + (0,) * len(shape),
                                      pipeline_mode=pl.Buffered(1))
    widths = (GQA_WIDTH, GQA_KV_WIDTH, GQA_KV_HEADS * V_EXT,
              MLA_QK_WIDTH, MLA_QK_WIDTH, MLA_HEADS * V_EXT, D_MIX)
    return pl.pallas_call(
        functools.partial(_in_proj_kernel, layer=layer),
        grid=(n_b, N_BLK),
        in_specs=_stream_specs(ctx_blk) + [
            pl.BlockSpec((None, None, 3, D_MODEL), lambda b, j: (layer, _mod_row(b, j), 0, 0)),
            lay(1, D_MODEL),
            pl.BlockSpec(memory_space=pl.ANY),
            lay(1, HEAD_DIM), lay(1, HEAD_DIM), lay(1, MLA_Q_RANK), lay(1, MLA_KV_RANK),
            lay(MLA_Q_RANK, MLA_QK_WIDTH),
            lay(MLA_KV_RANK, MLA_HEADS * (MLA_NOPE + MLA_V)),
            pl.BlockSpec((TM, TAB_WIDTH), lambda b, j: (j, 0)),
        ],
        out_specs=[tok(w) for w in widths],
        out_shape=[jax.ShapeDtypeStruct((n_b, T_TOK, w), BF16) for w in widths],
        scratch_shapes=[pltpu.VMEM((D_IN, D_MODEL), BF16),
                        pltpu.VMEM((W_SLOTS, W_CHUNK, D_MODEL), F32),
                        pltpu.SemaphoreType.DMA((W_SLOTS,))],
        compiler_params=pltpu.CompilerParams(
            dimension_semantics=("arbitrary", "arbitrary"), vmem_limit_bytes=VMEM_LIMIT),
        name="in_proj",
    )(*stream, mod, *params, tab)


def _attend(q_ref, k_ref, v_ref, g_ref, o_ref, *, kv_heads, group, dk, dv):
    for p in range(kv_heads):
        k = k_ref[:, p * dk:(p + 1) * dk]
        v = v_ref[:, p * V_EXT:(p + 1) * V_EXT]
        for r0 in range(0, q_ref.shape[0], TM):
            for i in range(group):
                hd = p * group + i
                q = q_ref[r0:r0 + TM, hd * dk:(hd + 1) * dk]
                s = lax.dot_general(q, k, (((1,), (1,)), ((), ())), preferred_element_type=F32)
                e = jnp.exp2(s - jnp.max(s, axis=-1, keepdims=True))
                o = jnp.dot(e.astype(BF16), v, preferred_element_type=F32)
                gate = g_ref[r0:r0 + TM, hd * dv:(hd + 1) * dv].astype(F32)
                o_ref[r0:r0 + TM, hd * dv:(hd + 1) * dv] = (
                    o[:, :dv] * (gate / o[:, dv:])).astype(BF16)


_GQA_CFG = dict(kv_heads=GQA_KV_HEADS, group=GQA_GROUP, dk=HEAD_DIM, dv=HEAD_DIM)


def _attn_gqa(q, k, v, g):
    n_b = q.shape[0]
    return pl.pallas_call(
        functools.partial(_attend, **_GQA_CFG),
        grid=(n_b, SEQ // TQ_GQA),
        in_specs=[
            pl.BlockSpec((None, TQ_GQA, GQA_WIDTH), lambda b, j: (b, j, 0)),
            pl.BlockSpec((None, T_TOK, GQA_KV_WIDTH), lambda b, j: (b, 0, 0)),
            pl.BlockSpec((None, T_TOK, GQA_KV_HEADS * V_EXT), lambda b, j: (b, 0, 0)),
            pl.BlockSpec((None, TQ_GQA, GQA_WIDTH), lambda b, j: (b, j, 0)),
        ],
        out_specs=pl.BlockSpec((None, TQ_GQA, GQA_WIDTH), lambda b, j: (b, j, 0)),
        out_shape=jax.ShapeDtypeStruct((n_b, SEQ, GQA_WIDTH), BF16),
        compiler_params=pltpu.CompilerParams(
            dimension_semantics=("arbitrary", "arbitrary"), vmem_limit_bytes=VMEM_LIMIT),
        name="attn_gqa",
    )(q, k, v, g)


def _attn_mla(q, k, v, g):
    n_b = q.shape[0]
    hps = MLA_HEADS_PER_STEP
    g_col0 = GQA_WIDTH // (hps * MLA_V)
    return pl.pallas_call(
        functools.partial(_attend, kv_heads=hps, group=1, dk=MLA_DK, dv=MLA_V),
        grid=(n_b, MLA_HEADS // hps, SEQ // TQ_MLA),
        in_specs=[
            pl.BlockSpec((None, TQ_MLA, hps * MLA_DK), lambda b, hg, j: (b, j, hg)),
            pl.BlockSpec((None, T_TOK, hps * MLA_DK), lambda b, hg, j: (b, 0, hg)),
            pl.BlockSpec((None, T_TOK, hps * V_EXT), lambda b, hg, j: (b, 0, hg)),
            pl.BlockSpec((None, TQ_MLA, hps * MLA_V), lambda b, hg, j: (b, j, g_col0 + hg)),
        ],
        out_specs=pl.BlockSpec((None, TQ_MLA, hps * MLA_V), lambda b, hg, j: (b, j, hg)),
        out_shape=jax.ShapeDtypeStruct((n_b, SEQ, MLA_WIDTH), BF16),
        compiler_params=pltpu.CompilerParams(
            dimension_semantics=("arbitrary", "arbitrary", "arbitrary"), vmem_limit_bytes=VMEM_LIMIT),
        name="attn_mla",
    )(q, k, v, g)


def _attn_ctx_kernel(qa_ref, ka_ref, va_ref, qm_ref, km_ref, vm_ref, g_ref, oa_ref, om_ref):
    _attend(qa_ref, ka_ref, va_ref, g_ref.at[:, :GQA_WIDTH], oa_ref, **_GQA_CFG)
    _attend(qm_ref, km_ref, vm_ref, g_ref.at[:, GQA_WIDTH:], om_ref,
            kv_heads=MLA_HEADS, group=1, dk=MLA_DK, dv=MLA_V)


def _attn_ctx(qa, ka, va, qm, km, vm, g):
    n_b = qa.shape[0]
    ctx_rows = lambda width: pl.BlockSpec((None, CTX_LEN, width), lambda b: (b, SEQ // CTX_LEN, 0))
    out = lambda width: pl.BlockSpec((None, CTX_LEN, width), lambda b: (b, 0, 0))
    return pl.pallas_call(
        _attn_ctx_kernel,
        grid=(n_b,),
        in_specs=[ctx_rows(GQA_WIDTH), ctx_rows(GQA_KV_WIDTH), ctx_rows(GQA_KV_HEADS * V_EXT),
                  ctx_rows(MLA_QK_WIDTH), ctx_rows(MLA_QK_WIDTH), ctx_rows(MLA_HEADS * V_EXT),
                  ctx_rows(D_MIX)],
        out_specs=[out(GQA_WIDTH), out(MLA_WIDTH)],
        out_shape=[jax.ShapeDtypeStruct((n_b, CTX_LEN, GQA_WIDTH), BF16),
                   jax.ShapeDtypeStruct((n_b, CTX_LEN, MLA_WIDTH), BF16)],
        compiler_params=pltpu.CompilerParams(
            dimension_semantics=("arbitrary",), vmem_limit_bytes=VMEM_LIMIT),
        name="attn_ctx",
    )(qa, ka, va, qm, km, vm, g)


def _out_proj_kernel(*refs, last, n_blk):
    if last:
        ya_ref, ym_ref, w_ref, x_hbm, mod_ref, fg_ref, o_ref, xbuf, sem = refs
    else:
        ya_ref, ym_ref, yca_ref, ycm_ref, w_ref, x_hbm, xc_ref, mod_ref, o_ref, xbuf, sem = refs

    def residual(a_ref, m_ref, x_ref):
        y = jnp.concatenate([a_ref[...], m_ref[...]], axis=-1)
        return x_ref[...] + mod_ref[2:3, :] * jnp.dot(y, w_ref[...], preferred_element_type=F32)

    j = pl.program_id(1)
    step = pl.program_id(0) * n_blk + j
    n_steps = pl.num_programs(0) * n_blk

    def fetch(s):
        s = jnp.asarray(s, jnp.int32)
        row0 = pl.multiple_of((s % n_blk) * TM_OUT, TM_OUT)
        slot = s % X_SLOTS
        return pltpu.make_async_copy(x_hbm.at[s // n_blk, pl.ds(row0, TM_OUT), :],
                                     xbuf.at[slot], sem.at[slot])

    def start_fetch(s):
        s = jnp.asarray(s, jnp.int32)
        pl.when(jnp.logical_and(s < n_steps, s % n_blk < N_LAT_OUT))(lambda: fetch(s).start())

    @pl.when(step == 0)
    def _():
        for s in range(X_SLOTS - 1):
            start_fetch(s)

    start_fetch(step + (X_SLOTS - 1))

    @pl.when(j < N_LAT_OUT)
    def _():
        fetch(step).wait()
        x_new = residual(ya_ref, ym_ref, xbuf.at[step % X_SLOTS])
        o_ref[...] = _rms(x_new) * fg_ref[...] if last else x_new

    if not last:
        @pl.when(j == N_LAT_OUT)
        def _():
            o_ref[:CTX_LEN, :] = residual(yca_ref, ycm_ref, xc_ref)
            o_ref[CTX_LEN:, :] = jnp.zeros((TM_OUT - CTX_LEN, D_MODEL), F32)


def _out_proj(layer, y_lat, y_ctx, w_out_b, stream, ctx_blk, mod, final_g):
    n_b = y_lat[0].shape[0]
    last = y_ctx is None
    n_blk = N_LAT_OUT if last else N_LAT_OUT + 1
    lat = lambda width: pl.BlockSpec((None, TM_OUT, width),
                                     lambda b, j: (b, jnp.minimum(j, N_LAT_OUT - 1), 0))
    ctx = lambda width, blk: pl.BlockSpec((None, CTX_LEN, width), lambda b, j: (b, blk, 0))
    w_spec = pl.BlockSpec((None, D_MIX, D_MODEL), lambda b, j: (layer, 0, 0),
                          pipeline_mode=pl.Buffered(1))
    mod_spec = pl.BlockSpec((None, None, 3, D_MODEL),
                            lambda b, j: (layer, jnp.where(j == N_LAT_OUT, ADA_CTX_ROW, b), 0, 0))
    x_hbm = pl.BlockSpec(memory_space=pl.ANY)
    if last:
        in_specs = [lat(GQA_WIDTH), lat(MLA_WIDTH), w_spec, x_hbm, mod_spec,
                    pl.BlockSpec((1, D_MODEL), lambda b, j: (0, 0))]
        args = (*y_lat, w_out_b, stream[0], mod, final_g.reshape(1, D_MODEL))
    else:
        in_specs = [lat(GQA_WIDTH), lat(MLA_WIDTH), ctx(GQA_WIDTH, 0), ctx(MLA_WIDTH, 0), w_spec,
                    x_hbm, ctx(D_MODEL, ctx_blk), mod_spec]
        args = (*y_lat, *y_ctx, w_out_b, *stream, mod)
    return pl.pallas_call(
        functools.partial(_out_proj_kernel, last=last, n_blk=n_blk),
        grid=(n_b, n_blk),
        in_specs=in_specs,
        out_specs=pl.BlockSpec((None, TM_OUT, D_MODEL), lambda b, j: (b, j, 0)),
        out_shape=jax.ShapeDtypeStruct((n_b, n_blk * TM_OUT, D_MODEL), F32),
        scratch_shapes=[pltpu.VMEM((X_SLOTS, TM_OUT, D_MODEL), F32),
                        pltpu.SemaphoreType.DMA((X_SLOTS,))],
        compiler_params=pltpu.CompilerParams(
            dimension_semantics=("arbitrary", "arbitrary"), vmem_limit_bytes=VMEM_LIMIT),
        name="out_proj",
    )(*args)


def _pack_w_uq(w_uq):
    w = w_uq.reshape(DEPTH, MLA_Q_RANK, MLA_HEADS, MLA_QK)
    w = jnp.pad(w, ((0, 0), (0, 0), (0, 0), (0, MLA_DK - MLA_QK)))
    return w.reshape(DEPTH, MLA_Q_RANK, MLA_QK_WIDTH).astype(BF16)


def _rope_table():
    f32 = np.float32
    rows = SEQ // GRID_W
    row = np.repeat(np.arange(rows, dtype=f32), GRID_W)
    col = np.tile(np.arange(GRID_W, dtype=f32), rows)

    def cos_sin(rot_dim):
        n_freq = rot_dim // 4
        inv = (f32(ROPE_THETA) ** (-np.arange(n_freq, dtype=f32) / f32(n_freq))).astype(f32)
        ang = np.concatenate([row[:, None] * inv[None], col[:, None] * inv[None]], axis=-1)
        return np.cos(ang).astype(f32), np.sin(ang).astype(f32)

    cos, sin = cos_sin(HEAD_DIM)
    cos_m, sin_m = cos_sin(MLA_ROPE)
    z32 = np.zeros_like(sin_m)
    z64 = np.zeros_like(sin)
    lat = np.concatenate([
        cos, cos,
        -sin, sin,
        cos_m, cos_m, z64,
        -sin_m, z32, z64,
        z32, sin_m, z64,
    ], axis=-1)
    ctx_row = np.concatenate([np.ones(LANES, f32), np.zeros(LANES, f32),
                              np.ones(LANES, f32), np.zeros(2 * LANES, f32)])
    return jnp.asarray(np.concatenate([lat, np.broadcast_to(ctx_row, (CTX_LEN, TAB_WIDTH))], axis=0))


def kernel(x, c, ctx, c_ctx, w_ada, b_ada, norm_g, w_in, q_gain, k_gain, cq_gain, ckv_gain,
           w_uq, w_ukv, w_out, final_g):
    n_b = x.shape[0]
    assert x.shape == (n_b, SEQ, D_MODEL) and ctx.shape == (n_b, CTX_LEN, D_MODEL)
    assert n_b == ADA_CTX_ROW and w_ada.shape[0] == DEPTH and w_in.shape[-1] == D_IN

    c_all = jnp.concatenate(
        [c, c_ctx[None], jnp.zeros((ADA_ROWS - n_b - 1, D_MODEL), F32)], axis=0)
    mod = _ada_rows(c_all, w_ada, b_ada).reshape(DEPTH, ADA_ROWS, 3, D_MODEL)

    row3 = lambda a: a.reshape(DEPTH, 1, a.shape[-1])
    params = (row3(norm_g), jnp.swapaxes(w_in, 1, 2), row3(q_gain), row3(k_gain), row3(cq_gain),
              row3(ckv_gain), _pack_w_uq(w_uq), w_ukv.astype(BF16))
    w_out_b = w_out.astype(BF16)
    tab = _rope_table()

    stream, ctx_blk = (x, ctx), 0
    for layer in range(DEPTH):
        last = layer == DEPTH - 1
        qa, ka, va, qm, km, vm, g = _in_proj(layer, stream, ctx_blk, mod, params, tab)
        y_lat = (_attn_gqa(qa, ka, va, g), _attn_mla(qm, km, vm, g))
        y_ctx = None if last else _attn_ctx(qa, ka, va, qm, km, vm, g)
        xs = _out_proj(layer, y_lat, y_ctx, w_out_b, stream, ctx_blk, mod, final_g)
        stream, ctx_blk = (xs, xs), SEQ // CTX_LEN
    return xs
```

```python
import functools

import jax
import jax.numpy as jnp
import numpy as np
from jax import lax
from jax.experimental import pallas as pl
from jax.experimental.pallas import tpu as pltpu

F32 = jnp.float32
BF16 = jnp.bfloat16

D_MODEL = 2048
SEQ = 2048
CTX_LEN = 256
T_TOK = SEQ + CTX_LEN
GRID_W = 64
ROPE_THETA = 10000.0
EPS = 1e-6
DEPTH = 4
LOG2_E = 1.4426950408889634

LANES = 128
HEAD_DIM = 128
GQA_HEADS = 8
GQA_KV_HEADS = 2
GQA_GROUP = GQA_HEADS // GQA_KV_HEADS
GQA_WIDTH = GQA_HEADS * HEAD_DIM
GQA_KV_WIDTH = GQA_KV_HEADS * HEAD_DIM

MLA_HEADS = 8
MLA_NOPE = 128
MLA_ROPE = 64
MLA_QK = MLA_NOPE + MLA_ROPE
MLA_V = 128
MLA_Q_RANK = 512
MLA_KV_RANK = 512
MLA_WIDTH = MLA_HEADS * MLA_V
D_MIX = GQA_WIDTH + MLA_WIDTH
MLA_DK = MLA_NOPE + LANES
MLA_QK_WIDTH = MLA_HEADS * MLA_DK
V_EXT = 2 * HEAD_DIM

COL_Q = 0
COL_K = COL_Q + GQA_WIDTH
COL_V = COL_K + GQA_KV_WIDTH
COL_GG = COL_V + GQA_KV_WIDTH
COL_CQ = COL_GG + GQA_WIDTH
COL_CKV = COL_CQ + MLA_Q_RANK
COL_KR = COL_CKV + MLA_KV_RANK
COL_GM = COL_KR + MLA_ROPE
D_IN = COL_GM + MLA_WIDTH

TAB_COS_A, TAB_SIN_A, TAB_COS_M, TAB_SIN_M_LO, TAB_SIN_M_HI = range(5)
TAB_WIDTH = 5 * LANES

ADA_ROWS = 8
ADA_CTX_ROW = 4
ADA_TN = 1024

TM = 256
TQ_GQA = 512
TQ_MLA = 1024
N_BLK = T_TOK // TM
N_LAT_BLK = SEQ // TM
TM_OUT = 512
N_LAT_OUT = SEQ // TM_OUT
X_SLOTS = 3
W_CHUNK = 64
W_SLOTS = 8
MLA_HEADS_PER_STEP = 4

VMEM_LIMIT = 56 * 1024 * 1024

assert CTX_LEN == TM and SEQ % TQ_GQA == 0 and SEQ % TQ_MLA == 0


def _silu(x):
    return x / (1.0 + jnp.exp(-x))


def _rms(x):
    return x * lax.rsqrt(jnp.mean(x * x, axis=-1, keepdims=True) + EPS)


def _tab(tab_ref, col):
    return tab_ref[:, col * LANES:(col + 1) * LANES]


def _rot_a(x, tab_ref):
    return x * _tab(tab_ref, TAB_COS_A) + pltpu.roll(x, HEAD_DIM // 2, 1) * _tab(tab_ref, TAB_SIN_A)


def _rot_m(x, tab_ref, scale=None):
    half = MLA_ROPE // 2
    y = (x * _tab(tab_ref, TAB_COS_M)
         + pltpu.roll(x, LANES - half, 1) * _tab(tab_ref, TAB_SIN_M_LO)
         + pltpu.roll(x, half, 1) * _tab(tab_ref, TAB_SIN_M_HI))
    return y if scale is None else y * scale


def _mod_row(b, j):
    return jnp.where(j == N_LAT_BLK, ADA_CTX_ROW, b)


def _stream_specs(ctx_blk):
    return [pl.BlockSpec((None, TM, D_MODEL), lambda b, j: (b, jnp.minimum(j, N_LAT_BLK - 1), 0)),
            pl.BlockSpec((None, CTX_LEN, D_MODEL), lambda b, j: (b, ctx_blk, 0))]


def _stream_block(xl_ref, xc_ref):
    return jnp.where(pl.program_id(1) == N_LAT_BLK, xc_ref[...], xl_ref[...])


def _ada_kernel(c_ref, w_ref, b_ref, o_ref):
    s = _silu(c_ref[...]).astype(BF16)
    w = w_ref[...].astype(BF16)
    o_ref[...] = jnp.dot(s, w, preferred_element_type=F32) + b_ref[...]


def _ada_rows(c_all, w_ada, b_ada):
    n_out = 3 * D_MODEL
    return pl.pallas_call(
        _ada_kernel,
        grid=(DEPTH, n_out // ADA_TN),
        in_specs=[
            pl.BlockSpec((ADA_ROWS, D_MODEL), lambda l, n: (0, 0)),
            pl.BlockSpec((None, D_MODEL, ADA_TN), lambda l, n: (l, 0, n)),
            pl.BlockSpec((None, 1, ADA_TN), lambda l, n: (l, 0, n)),
        ],
        out_specs=pl.BlockSpec((None, ADA_ROWS, ADA_TN), lambda l, n: (l, 0, n)),
        out_shape=jax.ShapeDtypeStruct((DEPTH, ADA_ROWS, n_out), F32),
        compiler_params=pltpu.CompilerParams(
            dimension_semantics=("arbitrary", "arbitrary"), vmem_limit_bytes=VMEM_LIMIT),
        name="ada_rows",
    )(c_all, w_ada, b_ada.reshape(DEPTH, 1, n_out))


def _store_bf16(dst_ref, rows, src_ref):
    dst_ref[rows, :] = src_ref[...].astype(BF16)


def _store_w_uq(dst_ref, rows, src_ref):
    low_half = lax.broadcasted_iota(jnp.int32, (W_CHUNK, LANES), 1) < MLA_ROPE
    for i in range(MLA_HEADS):
        c0 = i * MLA_QK
        tail = src_ref[:, c0 + MLA_QK - LANES:c0 + MLA_QK]
        rope = jnp.where(low_half, pltpu.roll(tail, MLA_ROPE, 1), 0.0)
        dst_ref[rows, i * MLA_DK:i * MLA_DK + MLA_NOPE] = src_ref[:, c0:c0 + MLA_NOPE].astype(BF16)
        dst_ref[rows, i * MLA_DK + MLA_NOPE:(i + 1) * MLA_DK] = rope.astype(BF16)


def _load_weight(w_hbm, layer, dst_ref, stage, sem, store=_store_bf16):
    n_chunks = w_hbm.shape[1] // W_CHUNK

    def rows(c):
        row0 = c * W_CHUNK
        return pl.ds(row0 if isinstance(c, int) else pl.multiple_of(row0, W_CHUNK), W_CHUNK)

    def chunk(c):
        slot = c % W_SLOTS
        return pltpu.make_async_copy(w_hbm.at[layer, rows(c), :], stage.at[slot], sem.at[slot])

    for c in range(W_SLOTS):
        chunk(c).start()

    def convert(c, carry):
        chunk(c).wait()
        store(dst_ref, rows(c), stage.at[c % W_SLOTS])
        pl.when(c + W_SLOTS < n_chunks)(lambda: chunk(c + W_SLOTS).start())
        return carry

    lax.fori_loop(0, n_chunks, convert, 0)


def _first_step():
    return jnp.logical_and(pl.program_id(0) == 0, pl.program_id(1) == 0)


def _in_proj_kernel(xl_ref, xc_ref, mod_ref, ng_ref, w_in_hbm, qg_ref, kg_ref, cqg_ref, ckvg_ref,
                    w_uq_hbm, w_ukv_hbm, tab_ref,
                    qa_ref, ka_ref, va_ref, qm_ref, km_ref, vm_ref, g_ref,
                    w_in_ref, w_uq_ref, w_ukv_ref, stage, stage_uq, sem, *, layer):
    @pl.when(_first_step())
    def _():
        _load_weight(w_in_hbm, layer, w_in_ref, stage, sem)
        _load_weight(w_ukv_hbm, layer, w_ukv_ref, stage, sem)
        _load_weight(w_uq_hbm, layer, w_uq_ref, stage_uq, sem, _store_w_uq)

    x = _stream_block(xl_ref, xc_ref)
    h = _rms(x) * ng_ref[...] * (1.0 + mod_ref[1:2, :]) + mod_ref[0:1, :]
    hb = h.astype(BF16)
    proj = lambda r0, r1: lax.dot_general(hb, w_in_ref[r0:r1, :], (((1,), (1,)), ((), ())),
                                          preferred_element_type=F32)
    c = proj(COL_CQ, COL_KR)
    cq = (_rms(c[:, :MLA_Q_RANK]) * cqg_ref[...]).astype(BF16)
    ckv = (_rms(c[:, MLA_Q_RANK:]) * ckvg_ref[...]).astype(BF16)

    p = proj(COL_Q, COL_GG)
    q_gain = qg_ref[...] * (HEAD_DIM ** -0.5 * LOG2_E)
    for i in range(GQA_HEADS):
        c0 = COL_Q + i * HEAD_DIM
        qh = _rms(p[:, c0:c0 + HEAD_DIM]) * q_gain
        qa_ref[:, i * HEAD_DIM:(i + 1) * HEAD_DIM] = _rot_a(qh, tab_ref).astype(BF16)
    k_gain = kg_ref[...]
    for i in range(GQA_KV_HEADS):
        c0 = COL_K + i * HEAD_DIM
        kh = _rms(p[:, c0:c0 + HEAD_DIM]) * k_gain
        ka_ref[:, i * HEAD_DIM:(i + 1) * HEAD_DIM] = _rot_a(kh, tab_ref).astype(BF16)
    ones = jnp.ones((TM, HEAD_DIM), BF16)
    for i in range(GQA_KV_HEADS):
        c0 = COL_V + i * HEAD_DIM
        va_ref[:, i * V_EXT:i * V_EXT + HEAD_DIM] = p[:, c0:c0 + HEAD_DIM].astype(BF16)
        va_ref[:, i * V_EXT + HEAD_DIM:(i + 1) * V_EXT] = ones

    g_ref[:, :GQA_WIDTH] = _silu(proj(COL_GG, COL_CQ)).astype(BF16)
    g_ref[:, GQA_WIDTH:] = _silu(proj(COL_GM, D_IN)).astype(BF16)
    kr_in = proj(COL_KR, COL_KR + LANES)
    kr_in = jnp.where(lax.broadcasted_iota(jnp.int32, kr_in.shape, 1) < MLA_ROPE, kr_in, 0.0)

    qm = jnp.dot(cq, w_uq_ref[...], preferred_element_type=F32)
    m_scale = MLA_QK ** -0.5 * LOG2_E
    for i in range(MLA_HEADS):
        c0 = i * MLA_DK
        qm_ref[:, c0:c0 + MLA_NOPE] = (qm[:, c0:c0 + MLA_NOPE] * m_scale).astype(BF16)
        qm_ref[:, c0 + MLA_NOPE:c0 + MLA_DK] = _rot_m(
            qm[:, c0 + MLA_NOPE:c0 + MLA_DK], tab_ref, m_scale).astype(BF16)

    kvm = jnp.dot(ckv, w_ukv_ref[...], preferred_element_type=F32)
    kr = _rot_m(kr_in, tab_ref).astype(BF16)
    for i in range(MLA_HEADS):
        c0 = i * (MLA_NOPE + MLA_V)
        km_ref[:, i * MLA_DK:i * MLA_DK + MLA_NOPE] = kvm[:, c0:c0 + MLA_NOPE].astype(BF16)
        km_ref[:, i * MLA_DK + MLA_NOPE:(i + 1) * MLA_DK] = kr
        vm_ref[:, i * V_EXT:i * V_EXT + MLA_V] = kvm[:, c0 + MLA_NOPE:c0 + MLA_NOPE + MLA_V].astype(BF16)
        vm_ref[:, i * V_EXT + MLA_V:(i + 1) * V_EXT] = ones


def _in_proj(layer, stream, ctx_blk, mod, params, tab):
    n_b = stream[0].shape[0]
    tok = lambda width: pl.BlockSpec((None, TM, width), lambda b, j: (b, j, 0))
    lay = lambda *shape: pl.BlockSpec((None,) + shape, lambda b, j: (layer,) + (0,) * len(shape),
                                      pipeline_mode=pl.Buffered(1))
    f32_hbm = pl.BlockSpec(memory_space=pl.ANY)
    widths = (GQA_WIDTH, GQA_KV_WIDTH, GQA_KV_HEADS * V_EXT,
              MLA_QK_WIDTH, MLA_QK_WIDTH, MLA_HEADS * V_EXT, D_MIX)
    return pl.pallas_call(
        functools.partial(_in_proj_kernel, layer=layer),
        grid=(n_b, N_BLK),
        in_specs=_stream_specs(ctx_blk) + [
            pl.BlockSpec((None, None, 3, D_MODEL), lambda b, j: (layer, _mod_row(b, j), 0, 0)),
            lay(1, D_MODEL),
            f32_hbm,
            lay(1, HEAD_DIM), lay(1, HEAD_DIM), lay(1, MLA_Q_RANK), lay(1, MLA_KV_RANK),
            f32_hbm,
            f32_hbm,
            pl.BlockSpec((TM, TAB_WIDTH), lambda b, j: (j, 0)),
        ],
        out_specs=[tok(w) for w in widths],
        out_shape=[jax.ShapeDtypeStruct((n_b, T_TOK, w), BF16) for w in widths],
        scratch_shapes=[pltpu.VMEM((D_IN, D_MODEL), BF16),
                        pltpu.VMEM((MLA_Q_RANK, MLA_QK_WIDTH), BF16),
                        pltpu.VMEM((MLA_KV_RANK, MLA_HEADS * (MLA_NOPE + MLA_V)), BF16),
                        pltpu.VMEM((W_SLOTS, W_CHUNK, D_MODEL), F32),
                        pltpu.VMEM((W_SLOTS, W_CHUNK, MLA_HEADS * MLA_QK), F32),
                        pltpu.SemaphoreType.DMA((W_SLOTS,))],
        compiler_params=pltpu.CompilerParams(
            dimension_semantics=("arbitrary", "arbitrary"), vmem_limit_bytes=VMEM_LIMIT),
        name="in_proj",
    )(*stream, mod, *params, tab)


def _attend(q_ref, k_ref, v_ref, g_ref, o_ref, *, kv_heads, group, dk, dv):
    for p in range(kv_heads):
        k = k_ref[:, p * dk:(p + 1) * dk]
        v = v_ref[:, p * V_EXT:(p + 1) * V_EXT]
        for r0 in range(0, q_ref.shape[0], TM):
            for i in range(group):
                hd = p * group + i
                q = q_ref[r0:r0 + TM, hd * dk:(hd + 1) * dk]
                s = lax.dot_general(q, k, (((1,), (1,)), ((), ())), preferred_element_type=F32)
                e = jnp.exp2(s - jnp.max(s, axis=-1, keepdims=True))
                o = jnp.dot(e.astype(BF16), v, preferred_element_type=F32)
                gate = g_ref[r0:r0 + TM, hd * dv:(hd + 1) * dv].astype(F32)
                o_ref[r0:r0 + TM, hd * dv:(hd + 1) * dv] = (
                    o[:, :dv] * (gate / o[:, dv:])).astype(BF16)


_GQA_CFG = dict(kv_heads=GQA_KV_HEADS, group=GQA_GROUP, dk=HEAD_DIM, dv=HEAD_DIM)


def _attn_gqa(q, k, v, g):
    n_b = q.shape[0]
    return pl.pallas_call(
        functools.partial(_attend, **_GQA_CFG),
        grid=(n_b, SEQ // TQ_GQA),
        in_specs=[
            pl.BlockSpec((None, TQ_GQA, GQA_WIDTH), lambda b, j: (b, j, 0)),
            pl.BlockSpec((None, T_TOK, GQA_KV_WIDTH), lambda b, j: (b, 0, 0)),
            pl.BlockSpec((None, T_TOK, GQA_KV_HEADS * V_EXT), lambda b, j: (b, 0, 0)),
            pl.BlockSpec((None, TQ_GQA, GQA_WIDTH), lambda b, j: (b, j, 0)),
        ],
        out_specs=pl.BlockSpec((None, TQ_GQA, GQA_WIDTH), lambda b, j: (b, j, 0)),
        out_shape=jax.ShapeDtypeStruct((n_b, SEQ, GQA_WIDTH), BF16),
        compiler_params=pltpu.CompilerParams(
            dimension_semantics=("arbitrary", "arbitrary"), vmem_limit_bytes=VMEM_LIMIT),
        name="attn_gqa",
    )(q, k, v, g)


def _attn_mla(q, k, v, g):
    n_b = q.shape[0]
    hps = MLA_HEADS_PER_STEP
    g_col0 = GQA_WIDTH // (hps * MLA_V)
    return pl.pallas_call(
        functools.partial(_attend, kv_heads=hps, group=1, dk=MLA_DK, dv=MLA_V),
        grid=(n_b, MLA_HEADS // hps, SEQ // TQ_MLA),
        in_specs=[
            pl.BlockSpec((None, TQ_MLA, hps * MLA_DK), lambda b, hg, j: (b, j, hg)),
            pl.BlockSpec((None, T_TOK, hps * MLA_DK), lambda b, hg, j: (b, 0, hg)),
            pl.BlockSpec((None, T_TOK, hps * V_EXT), lambda b, hg, j: (b, 0, hg)),
            pl.BlockSpec((None, TQ_MLA, hps * MLA_V), lambda b, hg, j: (b, j, g_col0 + hg)),
        ],
        out_specs=pl.BlockSpec((None, TQ_MLA, hps * MLA_V), lambda b, hg, j: (b, j, hg)),
        out_shape=jax.ShapeDtypeStruct((n_b, SEQ, MLA_WIDTH), BF16),
        compiler_params=pltpu.CompilerParams(
            dimension_semantics=("arbitrary", "arbitrary", "arbitrary"), vmem_limit_bytes=VMEM_LIMIT),
        name="attn_mla",
    )(q, k, v, g)


def _attn_ctx_kernel(qa_ref, ka_ref, va_ref, qm_ref, km_ref, vm_ref, g_ref, oa_ref, om_ref):
    _attend(qa_ref, ka_ref, va_ref, g_ref.at[:, :GQA_WIDTH], oa_ref, **_GQA_CFG)
    _attend(qm_ref, km_ref, vm_ref, g_ref.at[:, GQA_WIDTH:], om_ref,
            kv_heads=MLA_HEADS, group=1, dk=MLA_DK, dv=MLA_V)


def _attn_ctx(qa, ka, va, qm, km, vm, g):
    n_b = qa.shape[0]
    ctx_rows = lambda width: pl.BlockSpec((None, CTX_LEN, width), lambda b: (b, SEQ // CTX_LEN, 0))
    out = lambda width: pl.BlockSpec((None, CTX_LEN, width), lambda b: (b, 0, 0))
    return pl.pallas_call(
        _attn_ctx_kernel,
        grid=(n_b,),
        in_specs=[ctx_rows(GQA_WIDTH), ctx_rows(GQA_KV_WIDTH), ctx_rows(GQA_KV_HEADS * V_EXT),
                  ctx_rows(MLA_QK_WIDTH), ctx_rows(MLA_QK_WIDTH), ctx_rows(MLA_HEADS * V_EXT),
                  ctx_rows(D_MIX)],
        out_specs=[out(GQA_WIDTH), out(MLA_WIDTH)],
        out_shape=[jax.ShapeDtypeStruct((n_b, CTX_LEN, GQA_WIDTH), BF16),
                   jax.ShapeDtypeStruct((n_b, CTX_LEN, MLA_WIDTH), BF16)],
        compiler_params=pltpu.CompilerParams(
            dimension_semantics=("arbitrary",), vmem_limit_bytes=VMEM_LIMIT),
        name="attn_ctx",
    )(qa, ka, va, qm, km, vm, g)


def _out_proj_kernel(*refs, layer, last, n_blk):
    refs, (w_ref, xbuf, stage, sem, wsem) = refs[:-5], refs[-5:]
    if last:
        ya_ref, ym_ref, w_hbm, x_hbm, mod_ref, fg_ref, o_ref = refs
    else:
        ya_ref, ym_ref, yca_ref, ycm_ref, w_hbm, x_hbm, xc_ref, mod_ref, o_ref = refs

    def residual(a_ref, m_ref, x_ref):
        y = jnp.concatenate([a_ref[...], m_ref[...]], axis=-1)
        return x_ref[...] + mod_ref[2:3, :] * jnp.dot(y, w_ref[...], preferred_element_type=F32)

    j = pl.program_id(1)
    step = pl.program_id(0) * n_blk + j
    n_steps = pl.num_programs(0) * n_blk

    def fetch(s):
        s = jnp.asarray(s, jnp.int32)
        row0 = pl.multiple_of((s % n_blk) * TM_OUT, TM_OUT)
        slot = s % X_SLOTS
        return pltpu.make_async_copy(x_hbm.at[s // n_blk, pl.ds(row0, TM_OUT), :],
                                     xbuf.at[slot], sem.at[slot])

    def start_fetch(s):
        s = jnp.asarray(s, jnp.int32)
        pl.when(jnp.logical_and(s < n_steps, s % n_blk < N_LAT_OUT))(lambda: fetch(s).start())

    @pl.when(step == 0)
    def _():
        for s in range(X_SLOTS - 1):
            start_fetch(s)
        _load_weight(w_hbm, layer, w_ref, stage, wsem)

    start_fetch(step + (X_SLOTS - 1))

    @pl.when(j < N_LAT_OUT)
    def _():
        fetch(step).wait()
        x_new = residual(ya_ref, ym_ref, xbuf.at[step % X_SLOTS])
        o_ref[...] = _rms(x_new) * fg_ref[...] if last else x_new

    if not last:
        @pl.when(j == N_LAT_OUT)
        def _():
            o_ref[:CTX_LEN, :] = residual(yca_ref, ycm_ref, xc_ref)
            o_ref[CTX_LEN:, :] = jnp.zeros((TM_OUT - CTX_LEN, D_MODEL), F32)


def _out_proj(layer, y_lat, y_ctx, w_out_b, stream, ctx_blk, mod, final_g):
    n_b = y_lat[0].shape[0]
    last = y_ctx is None
    n_blk = N_LAT_OUT if last else N_LAT_OUT + 1
    lat = lambda width: pl.BlockSpec((None, TM_OUT, width),
                                     lambda b, j: (b, jnp.minimum(j, N_LAT_OUT - 1), 0))
    ctx = lambda width, blk: pl.BlockSpec((None, CTX_LEN, width), lambda b, j: (b, blk, 0))
    w_spec = pl.BlockSpec(memory_space=pl.ANY)
    mod_spec = pl.BlockSpec((None, None, 3, D_MODEL),
                            lambda b, j: (layer, jnp.where(j == N_LAT_OUT, ADA_CTX_ROW, b), 0, 0))
    x_hbm = pl.BlockSpec(memory_space=pl.ANY)
    if last:
        in_specs = [lat(GQA_WIDTH), lat(MLA_WIDTH), w_spec, x_hbm, mod_spec,
                    pl.BlockSpec((1, D_MODEL), lambda b, j: (0, 0))]
        args = (*y_lat, w_out_b, stream[0], mod, final_g.reshape(1, D_MODEL))
    else:
        in_specs = [lat(GQA_WIDTH), lat(MLA_WIDTH), ctx(GQA_WIDTH, 0), ctx(MLA_WIDTH, 0), w_spec,
                    x_hbm, ctx(D_MODEL, ctx_blk), mod_spec]
        args = (*y_lat, *y_ctx, w_out_b, *stream, mod)
    return pl.pallas_call(
        functools.partial(_out_proj_kernel, layer=layer, last=last, n_blk=n_blk),
        grid=(n_b, n_blk),
        in_specs=in_specs,
        out_specs=pl.BlockSpec((None, TM_OUT, D_MODEL), lambda b, j: (b, j, 0)),
        out_shape=jax.ShapeDtypeStruct((n_b, n_blk * TM_OUT, D_MODEL), F32),
        scratch_shapes=[pltpu.VMEM((D_MIX, D_MODEL), BF16),
                        pltpu.VMEM((X_SLOTS, TM_OUT, D_MODEL), F32),
                        pltpu.VMEM((W_SLOTS, W_CHUNK, D_MODEL), F32),
                        pltpu.SemaphoreType.DMA((X_SLOTS,)),
                        pltpu.SemaphoreType.DMA((W_SLOTS,))],
        compiler_params=pltpu.CompilerParams(
            dimension_semantics=("arbitrary", "arbitrary"), vmem_limit_bytes=VMEM_LIMIT),
        name="out_proj",
    )(*args)


def _rope_table():
    f32 = np.float32
    rows = SEQ // GRID_W
    row = np.repeat(np.arange(rows, dtype=f32), GRID_W)
    col = np.tile(np.arange(GRID_W, dtype=f32), rows)

    def cos_sin(rot_dim):
        n_freq = rot_dim // 4
        inv = (f32(ROPE_THETA) ** (-np.arange(n_freq, dtype=f32) / f32(n_freq))).astype(f32)
        ang = np.concatenate([row[:, None] * inv[None], col[:, None] * inv[None]], axis=-1)
        return np.cos(ang).astype(f32), np.sin(ang).astype(f32)

    cos, sin = cos_sin(HEAD_DIM)
    cos_m, sin_m = cos_sin(MLA_ROPE)
    z32 = np.zeros_like(sin_m)
    z64 = np.zeros_like(sin)
    lat = np.concatenate([
        cos, cos,
        -sin, sin,
        cos_m, cos_m, z64,
        -sin_m, z32, z64,
        z32, sin_m, z64,
    ], axis=-1)
    ctx_row = np.concatenate([np.ones(LANES, f32), np.zeros(LANES, f32),
                              np.ones(LANES, f32), np.zeros(2 * LANES, f32)])
    return jnp.asarray(np.concatenate([lat, np.broadcast_to(ctx_row, (CTX_LEN, TAB_WIDTH))], axis=0))


def kernel(x, c, ctx, c_ctx, w_ada, b_ada, norm_g, w_in, q_gain, k_gain, cq_gain, ckv_gain,
           w_uq, w_ukv, w_out, final_g):
    n_b = x.shape[0]
    assert x.shape == (n_b, SEQ, D_MODEL) and ctx.shape == (n_b, CTX_LEN, D_MODEL)
    assert n_b == ADA_CTX_ROW and w_ada.shape[0] == DEPTH and w_in.shape[-1] == D_IN

    c_all = jnp.concatenate(
        [c, c_ctx[None], jnp.zeros((ADA_ROWS - n_b - 1, D_MODEL), F32)], axis=0)
    mod = _ada_rows(c_all, w_ada, b_ada).reshape(DEPTH, ADA_ROWS, 3, D_MODEL)

    row3 = lambda a: a.reshape(DEPTH, 1, a.shape[-1])
    params = (row3(norm_g), jnp.swapaxes(w_in, 1, 2), row3(q_gain), row3(k_gain), row3(cq_gain),
              row3(ckv_gain), w_uq, w_ukv)
    tab = _rope_table()

    stream, ctx_blk = (x, ctx), 0
    for layer in range(DEPTH):
        last = layer == DEPTH - 1
        qa, ka, va, qm, km, vm, g = _in_proj(layer, stream, ctx_blk, mod, params, tab)
        y_lat = (_attn_gqa(qa, ka, va, g), _attn_mla(qm, km, vm, g))
        y_ctx = None if last else _attn_ctx(qa, ka, va, qm, km, vm, g)
        xs = _out_proj(layer, y_lat, y_ctx, w_out, stream, ctx_blk, mod, final_g)
        stream, ctx_blk = (xs, xs), SEQ // CTX_LEN
    return xs
```

```python
import functools

import jax
import jax.numpy as jnp
import numpy as np
from jax import lax
from jax.experimental import pallas as pl
from jax.experimental.pallas import tpu as pltpu

F32 = jnp.float32
BF16 = jnp.bfloat16

D_MODEL = 2048
SEQ = 2048
CTX_LEN = 256
T_TOK = SEQ + CTX_LEN
GRID_W = 64
ROPE_THETA = 10000.0
EPS = 1e-6
DEPTH = 4
LOG2_E = 1.4426950408889634

LANES = 128
HEAD_DIM = 128
GQA_HEADS = 8
GQA_KV_HEADS = 2
GQA_GROUP = GQA_HEADS // GQA_KV_HEADS
GQA_WIDTH = GQA_HEADS * HEAD_DIM
GQA_KV_WIDTH = GQA_KV_HEADS * HEAD_DIM

MLA_HEADS = 8
MLA_NOPE = 128
MLA_ROPE = 64
MLA_QK = MLA_NOPE + MLA_ROPE
MLA_V = 128
MLA_Q_RANK = 512
MLA_KV_RANK = 512
MLA_WIDTH = MLA_HEADS * MLA_V
D_MIX = GQA_WIDTH + MLA_WIDTH
MLA_DK = MLA_NOPE + LANES
MLA_QK_WIDTH = MLA_HEADS * MLA_DK
V_EXT = 2 * HEAD_DIM

COL_Q = 0
COL_K = COL_Q + GQA_WIDTH
COL_V = COL_K + GQA_KV_WIDTH
COL_GG = COL_V + GQA_KV_WIDTH
COL_CQ = COL_GG + GQA_WIDTH
COL_CKV = COL_CQ + MLA_Q_RANK
COL_KR = COL_CKV + MLA_KV_RANK
COL_GM = COL_KR + MLA_ROPE
D_IN = COL_GM + MLA_WIDTH

TAB_COS_A, TAB_SIN_A, TAB_COS_M, TAB_SIN_M_LO, TAB_SIN_M_HI = range(5)
TAB_WIDTH = 5 * LANES

ADA_ROWS = 8
ADA_CTX_ROW = 4
ADA_TN = 1024

TM = 256
TQ_GQA = 512
TQ_MLA = 1024
N_BLK = T_TOK // TM
N_LAT_BLK = SEQ // TM
TM_OUT = 512
N_LAT_OUT = SEQ // TM_OUT
X_SLOTS = 3
W_CHUNK = 64
W_SLOTS = 8
MLA_HEADS_PER_STEP = 4

VMEM_LIMIT = 56 * 1024 * 1024

assert CTX_LEN == TM and SEQ % TQ_GQA == 0 and SEQ % TQ_MLA == 0


def _silu(x):
    return x / (1.0 + jnp.exp(-x))


def _rms(x):
    return x * lax.rsqrt(jnp.mean(x * x, axis=-1, keepdims=True) + EPS)


def _tab(tab_ref, col):
    return tab_ref[:, col * LANES:(col + 1) * LANES]


def _rot_a(x, tab_ref):
    return x * _tab(tab_ref, TAB_COS_A) + pltpu.roll(x, HEAD_DIM // 2, 1) * _tab(tab_ref, TAB_SIN_A)


def _rot_m(x, tab_ref, scale=None):
    half = MLA_ROPE // 2
    y = (x * _tab(tab_ref, TAB_COS_M)
         + pltpu.roll(x, LANES - half, 1) * _tab(tab_ref, TAB_SIN_M_LO)
         + pltpu.roll(x, half, 1) * _tab(tab_ref, TAB_SIN_M_HI))
    return y if scale is None else y * scale


def _mod_row(b, j):
    return jnp.where(j == N_LAT_BLK, ADA_CTX_ROW, b)


def _stream_specs(ctx_blk):
    return [pl.BlockSpec((None, TM, D_MODEL), lambda b, j: (b, jnp.minimum(j, N_LAT_BLK - 1), 0)),
            pl.BlockSpec((None, CTX_LEN, D_MODEL), lambda b, j: (b, ctx_blk, 0))]


def _stream_block(xl_ref, xc_ref):
    return jnp.where(pl.program_id(1) == N_LAT_BLK, xc_ref[...], xl_ref[...])


def _ada_kernel(c_ref, w_ref, b_ref, o_ref):
    s = _silu(c_ref[...]).astype(BF16)
    w = w_ref[...].astype(BF16)
    o_ref[...] = jnp.dot(s, w, preferred_element_type=F32) + b_ref[...]


def _ada_rows(c_all, w_ada, b_ada):
    n_out = 3 * D_MODEL
    return pl.pallas_call(
        _ada_kernel,
        grid=(DEPTH, n_out // ADA_TN),
        in_specs=[
            pl.BlockSpec((ADA_ROWS, D_MODEL), lambda l, n: (0, 0)),
            pl.BlockSpec((None, D_MODEL, ADA_TN), lambda l, n: (l, 0, n)),
            pl.BlockSpec((None, 1, ADA_TN), lambda l, n: (l, 0, n)),
        ],
        out_specs=pl.BlockSpec((None, ADA_ROWS, ADA_TN), lambda l, n: (l, 0, n)),
        out_shape=jax.ShapeDtypeStruct((DEPTH, ADA_ROWS, n_out), F32),
        compiler_params=pltpu.CompilerParams(
            dimension_semantics=("arbitrary", "arbitrary"), vmem_limit_bytes=VMEM_LIMIT),
        name="ada_rows",
    )(c_all, w_ada, b_ada.reshape(DEPTH, 1, n_out))


def _store_bf16(dst_ref, rows, src_ref):
    dst_ref[rows, :] = src_ref[...].astype(BF16)


def _store_w_uq(dst_ref, rows, src_ref):
    low_half = lax.broadcasted_iota(jnp.int32, (W_CHUNK, LANES), 1) < MLA_ROPE
    for i in range(MLA_HEADS):
        c0 = i * MLA_QK
        tail = src_ref[:, c0 + MLA_QK - LANES:c0 + MLA_QK]
        rope = jnp.where(low_half, pltpu.roll(tail, MLA_ROPE, 1), 0.0)
        dst_ref[rows, i * MLA_DK:i * MLA_DK + MLA_NOPE] = src_ref[:, c0:c0 + MLA_NOPE].astype(BF16)
        dst_ref[rows, i * MLA_DK + MLA_NOPE:(i + 1) * MLA_DK] = rope.astype(BF16)


def _load_weight(w_hbm, layer, dst_ref, stage, sem, store=_store_bf16):
    n_chunks = w_hbm.shape[1] // W_CHUNK

    def rows(c):
        row0 = c * W_CHUNK
        return pl.ds(row0 if isinstance(c, int) else pl.multiple_of(row0, W_CHUNK), W_CHUNK)

    def chunk(c):
        slot = c % W_SLOTS
        return pltpu.make_async_copy(w_hbm.at[layer, rows(c), :], stage.at[slot], sem.at[slot])

    for c in range(W_SLOTS):
        chunk(c).start()

    def convert(c, carry):
        chunk(c).wait()
        store(dst_ref, rows(c), stage.at[c % W_SLOTS])
        pl.when(c + W_SLOTS < n_chunks)(lambda: chunk(c + W_SLOTS).start())
        return carry

    lax.fori_loop(0, n_chunks, convert, 0)


def _first_step():
    return jnp.logical_and(pl.program_id(0) == 0, pl.program_id(1) == 0)


def _in_proj_kernel(xl_ref, xc_ref, mod_ref, ng_ref, w_in_hbm, qg_ref, kg_ref, cqg_ref, ckvg_ref,
                    w_uq_hbm, w_ukv_hbm, tab_ref,
                    qa_ref, ka_ref, va_ref, qm_ref, km_ref, vm_ref, g_ref,
                    w_in_ref, w_uq_ref, w_ukv_ref, stage, stage_uq, sem, *, layer):
    @pl.when(_first_step())
    def _():
        _load_weight(w_in_hbm, layer, w_in_ref, stage, sem)
        _load_weight(w_ukv_hbm, layer, w_ukv_ref, stage, sem)
        _load_weight(w_uq_hbm, layer, w_uq_ref, stage_uq, sem, _store_w_uq)

    x = _stream_block(xl_ref, xc_ref)
    h = _rms(x) * ng_ref[...] * (1.0 + mod_ref[1:2, :]) + mod_ref[0:1, :]
    hb = h.astype(BF16)
    proj = lambda r0, r1: lax.dot_general(hb, w_in_ref[r0:r1, :], (((1,), (1,)), ((), ())),
                                          preferred_element_type=F32)
    c = proj(COL_CQ, COL_KR)
    cq = (_rms(c[:, :MLA_Q_RANK]) * cqg_ref[...]).astype(BF16)
    ckv = (_rms(c[:, MLA_Q_RANK:]) * ckvg_ref[...]).astype(BF16)

    p = proj(COL_Q, COL_GG)
    q_gain = qg_ref[...] * (HEAD_DIM ** -0.5 * LOG2_E)
    for i in range(GQA_HEADS):
        c0 = COL_Q + i * HEAD_DIM
        qh = _rms(p[:, c0:c0 + HEAD_DIM]) * q_gain
        qa_ref[:, i * HEAD_DIM:(i + 1) * HEAD_DIM] = _rot_a(qh, tab_ref).astype(BF16)
    k_gain = kg_ref[...]
    for i in range(GQA_KV_HEADS):
        c0 = COL_K + i * HEAD_DIM
        kh = _rms(p[:, c0:c0 + HEAD_DIM]) * k_gain
        ka_ref[:, i * HEAD_DIM:(i + 1) * HEAD_DIM] = _rot_a(kh, tab_ref).astype(BF16)
    ones = jnp.ones((TM, HEAD_DIM), BF16)
    for i in range(GQA_KV_HEADS):
        c0 = COL_V + i * HEAD_DIM
        va_ref[:, i * V_EXT:i * V_EXT + HEAD_DIM] = p[:, c0:c0 + HEAD_DIM].astype(BF16)
        va_ref[:, i * V_EXT + HEAD_DIM:(i + 1) * V_EXT] = ones

    g_ref[:, :GQA_WIDTH] = _silu(proj(COL_GG, COL_CQ)).astype(BF16)
    g_ref[:, GQA_WIDTH:] = _silu(proj(COL_GM, D_IN)).astype(BF16)
    kr_in = proj(COL_KR, COL_KR + LANES)
    kr_in = jnp.where(lax.broadcasted_iota(jnp.int32, kr_in.shape, 1) < MLA_ROPE, kr_in, 0.0)

    qm = jnp.dot(cq, w_uq_ref[...], preferred_element_type=F32)
    m_scale = MLA_QK ** -0.5 * LOG2_E
    for i in range(MLA_HEADS):
        c0 = i * MLA_DK
        qm_ref[:, c0:c0 + MLA_NOPE] = (qm[:, c0:c0 + MLA_NOPE] * m_scale).astype(BF16)
        qm_ref[:, c0 + MLA_NOPE:c0 + MLA_DK] = _rot_m(
            qm[:, c0 + MLA_NOPE:c0 + MLA_DK], tab_ref, m_scale).astype(BF16)

    kvm = jnp.dot(ckv, w_ukv_ref[...], preferred_element_type=F32)
    kr = _rot_m(kr_in, tab_ref).astype(BF16)
    for i in range(MLA_HEADS):
        c0 = i * (MLA_NOPE + MLA_V)
        km_ref[:, i * MLA_DK:i * MLA_DK + MLA_NOPE] = kvm[:, c0:c0 + MLA_NOPE].astype(BF16)
        km_ref[:, i * MLA_DK + MLA_NOPE:(i + 1) * MLA_DK] = kr
        vm_ref[:, i * V_EXT:i * V_EXT + MLA_V] = kvm[:, c0 + MLA_NOPE:c0 + MLA_NOPE + MLA_V].astype(BF16)
        vm_ref[:, i * V_EXT + MLA_V:(i + 1) * V_EXT] = ones


def _in_proj(layer, stream, ctx_blk, mod, params, tab):
    n_b = stream[0].shape[0]
    tok = lambda width: pl.BlockSpec((None, TM, width), lambda b, j: (b, j, 0))
    lay = lambda *shape: pl.BlockSpec((None,) + shape, lambda b, j: (layer,) + (0,) * len(shape),
                                      pipeline_mode=pl.Buffered(1))
    f32_hbm = pl.BlockSpec(memory_space=pl.ANY)
    widths = (GQA_WIDTH, GQA_KV_WIDTH, GQA_KV_HEADS * V_EXT,
              MLA_QK_WIDTH, MLA_QK_WIDTH, MLA_HEADS * V_EXT, D_MIX)
    return pl.pallas_call(
        functools.partial(_in_proj_kernel, layer=layer),
        grid=(n_b, N_BLK),
        in_specs=_stream_specs(ctx_blk) + [
            pl.BlockSpec((None, None, 3, D_MODEL), lambda b, j: (layer, _mod_row(b, j), 0, 0)),
            lay(1, D_MODEL),
            f32_hbm,
            lay(1, HEAD_DIM), lay(1, HEAD_DIM), lay(1, MLA_Q_RANK), lay(1, MLA_KV_RANK),
            f32_hbm,
            f32_hbm,
            pl.BlockSpec((TM, TAB_WIDTH), lambda b, j: (j, 0)),
        ],
        out_specs=[tok(w) for w in widths],
        out_shape=[jax.ShapeDtypeStruct((n_b, T_TOK, w), BF16) for w in widths],
        scratch_shapes=[pltpu.VMEM((D_IN, D_MODEL), BF16),
                        pltpu.VMEM((MLA_Q_RANK, MLA_QK_WIDTH), BF16),
                        pltpu.VMEM((MLA_KV_RANK, MLA_HEADS * (MLA_NOPE + MLA_V)), BF16),
                        pltpu.VMEM((W_SLOTS, W_CHUNK, D_MODEL), F32),
                        pltpu.VMEM((W_SLOTS, W_CHUNK, MLA_HEADS * MLA_QK), F32),
                        pltpu.SemaphoreType.DMA((W_SLOTS,))],
        compiler_params=pltpu.CompilerParams(
            dimension_semantics=("arbitrary", "arbitrary"), vmem_limit_bytes=VMEM_LIMIT),
        name="in_proj",
    )(*stream, mod, *params, tab)


def _attend(q_ref, k_ref, v_ref, g_ref, o_ref, *, kv_heads, group, dk, dv):
    for p in range(kv_heads):
        k = k_ref[:, p * dk:(p + 1) * dk]
        v = v_ref[:, p * V_EXT:(p + 1) * V_EXT]
        for r0 in range(0, q_ref.shape[0], TM):
            for i in range(group):
                hd = p * group + i
                q = q_ref[r0:r0 + TM, hd * dk:(hd + 1) * dk]
                s = lax.dot_general(q, k, (((1,), (1,)), ((), ())), preferred_element_type=F32)
                e = jnp.exp2(s - jnp.max(s, axis=-1, keepdims=True))
                o = jnp.dot(e.astype(BF16), v, preferred_element_type=F32)
                gate = g_ref[r0:r0 + TM, hd * dv:(hd + 1) * dv].astype(F32)
                o_ref[r0:r0 + TM, hd * dv:(hd + 1) * dv] = (
                    o[:, :dv] * (gate / o[:, dv:])).astype(BF16)


_GQA_CFG = dict(kv_heads=GQA_KV_HEADS, group=GQA_GROUP, dk=HEAD_DIM, dv=HEAD_DIM)


def _attn_gqa(q, k, v, g):
    n_b = q.shape[0]
    return pl.pallas_call(
        functools.partial(_attend, **_GQA_CFG),
        grid=(n_b, SEQ // TQ_GQA),
        in_specs=[
            pl.BlockSpec((None, TQ_GQA, GQA_WIDTH), lambda b, j: (b, j, 0)),
            pl.BlockSpec((None, T_TOK, GQA_KV_WIDTH), lambda b, j: (b, 0, 0)),
            pl.BlockSpec((None, T_TOK, GQA_KV_HEADS * V_EXT), lambda b, j: (b, 0, 0)),
            pl.BlockSpec((None, TQ_GQA, GQA_WIDTH), lambda b, j: (b, j, 0)),
        ],
        out_specs=pl.BlockSpec((None, TQ_GQA, GQA_WIDTH), lambda b, j: (b, j, 0)),
        out_shape=jax.ShapeDtypeStruct((n_b, SEQ, GQA_WIDTH), BF16),
        compiler_params=pltpu.CompilerParams(
            dimension_semantics=("arbitrary", "arbitrary"), vmem_limit_bytes=VMEM_LIMIT),
        name="attn_gqa",
    )(q, k, v, g)


def _attn_mla(q, k, v, g):
    n_b = q.shape[0]
    hps = MLA_HEADS_PER_STEP
    g_col0 = GQA_WIDTH // (hps * MLA_V)
    return pl.pallas_call(
        functools.partial(_attend, kv_heads=hps, group=1, dk=MLA_DK, dv=MLA_V),
        grid=(n_b, MLA_HEADS // hps, SEQ // TQ_MLA),
        in_specs=[
            pl.BlockSpec((None, TQ_MLA, hps * MLA_DK), lambda b, hg, j: (b, j, hg)),
            pl.BlockSpec((None, T_TOK, hps * MLA_DK), lambda b, hg, j: (b, 0, hg)),
            pl.BlockSpec((None, T_TOK, hps * V_EXT), lambda b, hg, j: (b, 0, hg)),
            pl.BlockSpec((None, TQ_MLA, hps * MLA_V), lambda b, hg, j: (b, j, g_col0 + hg)),
        ],
        out_specs=pl.BlockSpec((None, TQ_MLA, hps * MLA_V), lambda b, hg, j: (b, j, hg)),
        out_shape=jax.ShapeDtypeStruct((n_b, SEQ, MLA_WIDTH), BF16),
        compiler_params=pltpu.CompilerParams(
            dimension_semantics=("arbitrary", "arbitrary", "arbitrary"), vmem_limit_bytes=VMEM_LIMIT),
        name="attn_mla",
    )(q, k, v, g)


def _attn_ctx_kernel(qa_ref, ka_ref, va_ref, qm_ref, km_ref, vm_ref, g_ref, oa_ref, om_ref):
    _attend(qa_ref, ka_ref, va_ref, g_ref.at[:, :GQA_WIDTH], oa_ref, **_GQA_CFG)
    _attend(qm_ref, km_ref, vm_ref, g_ref.at[:, GQA_WIDTH:], om_ref,
            kv_heads=MLA_HEADS, group=1, dk=MLA_DK, dv=MLA_V)


def _attn_ctx(qa, ka, va, qm, km, vm, g):
    n_b = qa.shape[0]
    ctx_rows = lambda width: pl.BlockSpec((None, CTX_LEN, width), lambda b: (b, SEQ // CTX_LEN, 0))
    out = lambda width: pl.BlockSpec((None, CTX_LEN, width), lambda b: (b, 0, 0))
    return pl.pallas_call(
        _attn_ctx_kernel,
        grid=(n_b,),
        in_specs=[ctx_rows(GQA_WIDTH), ctx_rows(GQA_KV_WIDTH), ctx_rows(GQA_KV_HEADS * V_EXT),
                  ctx_rows(MLA_QK_WIDTH), ctx_rows(MLA_QK_WIDTH), ctx_rows(MLA_HEADS * V_EXT),
                  ctx_rows(D_MIX)],
        out_specs=[out(GQA_WIDTH), out(MLA_WIDTH)],
        out_shape=[jax.ShapeDtypeStruct((n_b, CTX_LEN, GQA_WIDTH), BF16),
                   jax.ShapeDtypeStruct((n_b, CTX_LEN, MLA_WIDTH), BF16)],
        compiler_params=pltpu.CompilerParams(
            dimension_semantics=("arbitrary",), vmem_limit_bytes=VMEM_LIMIT),
        name="attn_ctx",
    )(qa, ka, va, qm, km, vm, g)


def _out_proj_kernel(*refs, layer, last, n_blk):
    refs, (w_ref, xbuf, abuf, mbuf, stage, sem, wsem) = refs[:-7], refs[-7:]
    if last:
        ya_hbm, ym_hbm, w_hbm, x_hbm, mod_ref, fg_ref, o_ref = refs
    else:
        ya_hbm, ym_hbm, yca_ref, ycm_ref, w_hbm, x_hbm, xc_ref, mod_ref, o_ref = refs

    def residual(a_ref, m_ref, x_ref):
        y = jnp.concatenate([a_ref[...], m_ref[...]], axis=-1)
        return x_ref[...] + mod_ref[2:3, :] * jnp.dot(y, w_ref[...], preferred_element_type=F32)

    j = pl.program_id(1)
    step = pl.program_id(0) * n_blk + j
    n_steps = pl.num_programs(0) * n_blk
    rings = ((x_hbm, xbuf), (ya_hbm, abuf), (ym_hbm, mbuf))

    def fetch(s):
        s = jnp.asarray(s, jnp.int32)
        row0 = pl.multiple_of((s % n_blk) * TM_OUT, TM_OUT)
        slot = s % X_SLOTS
        return [pltpu.make_async_copy(hbm.at[s // n_blk, pl.ds(row0, TM_OUT), :], buf.at[slot],
                                      sem.at[i, slot]) for i, (hbm, buf) in enumerate(rings)]

    def start_fetch(s):
        s = jnp.asarray(s, jnp.int32)

        @pl.when(jnp.logical_and(s < n_steps, s % n_blk < N_LAT_OUT))
        def _():
            for copy in fetch(s):
                copy.start()

    @pl.when(step == 0)
    def _():
        for s in range(X_SLOTS - 1):
            start_fetch(s)
        _load_weight(w_hbm, layer, w_ref, stage, wsem)

    start_fetch(step + (X_SLOTS - 1))

    @pl.when(j < N_LAT_OUT)
    def _():
        for copy in fetch(step):
            copy.wait()
        slot = step % X_SLOTS
        x_new = residual(abuf.at[slot], mbuf.at[slot], xbuf.at[slot])
        o_ref[...] = _rms(x_new) * fg_ref[...] if last else x_new

    if not last:
        @pl.when(j == N_LAT_OUT)
        def _():
            o_ref[:CTX_LEN, :] = residual(yca_ref, ycm_ref, xc_ref)
            o_ref[CTX_LEN:, :] = jnp.zeros((TM_OUT - CTX_LEN, D_MODEL), F32)


def _out_proj(layer, y_lat, y_ctx, w_out_b, stream, ctx_blk, mod, final_g):
    n_b = y_lat[0].shape[0]
    last = y_ctx is None
    n_blk = N_LAT_OUT if last else N_LAT_OUT + 1
    ctx = lambda width, blk: pl.BlockSpec((None, CTX_LEN, width), lambda b, j: (b, blk, 0))
    hbm = pl.BlockSpec(memory_space=pl.ANY)
    mod_spec = pl.BlockSpec((None, None, 3, D_MODEL),
                            lambda b, j: (layer, jnp.where(j == N_LAT_OUT, ADA_CTX_ROW, b), 0, 0))
    if last:
        in_specs = [hbm, hbm, hbm, hbm, mod_spec, pl.BlockSpec((1, D_MODEL), lambda b, j: (0, 0))]
        args = (*y_lat, w_out_b, stream[0], mod, final_g.reshape(1, D_MODEL))
    else:
        in_specs = [hbm, hbm, ctx(GQA_WIDTH, 0), ctx(MLA_WIDTH, 0), hbm,
                    hbm, ctx(D_MODEL, ctx_blk), mod_spec]
        args = (*y_lat, *y_ctx, w_out_b, *stream, mod)
    return pl.pallas_call(
        functools.partial(_out_proj_kernel, layer=layer, last=last, n_blk=n_blk),
        grid=(n_b, n_blk),
        in_specs=in_specs,
        out_specs=pl.BlockSpec((None, TM_OUT, D_MODEL), lambda b, j: (b, j, 0)),
        out_shape=jax.ShapeDtypeStruct((n_b, n_blk * TM_OUT, D_MODEL), F32),
        scratch_shapes=[pltpu.VMEM((D_MIX, D_MODEL), BF16),
                        pltpu.VMEM((X_SLOTS, TM_OUT, D_MODEL), F32),
                        pltpu.VMEM((X_SLOTS, TM_OUT, GQA_WIDTH), BF16),
                        pltpu.VMEM((X_SLOTS, TM_OUT, MLA_WIDTH), BF16),
                        pltpu.VMEM((W_SLOTS, W_CHUNK, D_MODEL), F32),
                        pltpu.SemaphoreType.DMA((3, X_SLOTS)),
                        pltpu.SemaphoreType.DMA((W_SLOTS,))],
        compiler_params=pltpu.CompilerParams(
            dimension_semantics=("arbitrary", "arbitrary"), vmem_limit_bytes=VMEM_LIMIT),
        name="out_proj",
    )(*args)


def _rope_table():
    f32 = np.float32
    rows = SEQ // GRID_W
    row = np.repeat(np.arange(rows, dtype=f32), GRID_W)
    col = np.tile(np.arange(GRID_W, dtype=f32), rows)

    def cos_sin(rot_dim):
        n_freq = rot_dim // 4
        inv = (f32(ROPE_THETA) ** (-np.arange(n_freq, dtype=f32) / f32(n_freq))).astype(f32)
        ang = np.concatenate([row[:, None] * inv[None], col[:, None] * inv[None]], axis=-1)
        return np.cos(ang).astype(f32), np.sin(ang).astype(f32)

    cos, sin = cos_sin(HEAD_DIM)
    cos_m, sin_m = cos_sin(MLA_ROPE)
    z32 = np.zeros_like(sin_m)
    z64 = np.zeros_like(sin)
    lat = np.concatenate([
        cos, cos,
        -sin, sin,
        cos_m, cos_m, z64,
        -sin_m, z32, z64,
        z32, sin_m, z64,
    ], axis=-1)
    ctx_row = np.concatenate([np.ones(LANES, f32), np.zeros(LANES, f32),
                              np.ones(LANES, f32), np.zeros(2 * LANES, f32)])
    return jnp.asarray(np.concatenate([lat, np.broadcast_to(ctx_row, (CTX_LEN, TAB_WIDTH))], axis=0))


def kernel(x, c, ctx, c_ctx, w_ada, b_ada, norm_g, w_in, q_gain, k_gain, cq_gain, ckv_gain,
           w_uq, w_ukv, w_out, final_g):
    n_b = x.shape[0]
    assert x.shape == (n_b, SEQ, D_MODEL) and ctx.shape == (n_b, CTX_LEN, D_MODEL)
    assert n_b == ADA_CTX_ROW and w_ada.shape[0] == DEPTH and w_in.shape[-1] == D_IN

    c_all = jnp.concatenate(
        [c, c_ctx[None], jnp.zeros((ADA_ROWS - n_b - 1, D_MODEL), F32)], axis=0)
    mod = _ada_rows(c_all, w_ada, b_ada).reshape(DEPTH, ADA_ROWS, 3, D_MODEL)

    row3 = lambda a: a.reshape(DEPTH, 1, a.shape[-1])
    params = (row3(norm_g), jnp.swapaxes(w_in, 1, 2), row3(q_gain), row3(k_gain), row3(cq_gain),
              row3(ckv_gain), w_uq, w_ukv)
    tab = _rope_table()

    stream, ctx_blk = (x, ctx), 0
    for layer in range(DEPTH):
        last = layer == DEPTH - 1
        qa, ka, va, qm, km, vm, g = _in_proj(layer, stream, ctx_blk, mod, params, tab)
        y_lat = (_attn_gqa(qa, ka, va, g), _attn_mla(qm, km, vm, g))
        y_ctx = None if last else _attn_ctx(qa, ka, va, qm, km, vm, g)
        xs = _out_proj(layer, y_lat, y_ctx, w_out, stream, ctx_blk, mod, final_g)
        stream, ctx_blk = (xs, xs), SEQ // CTX_LEN
    return xs
```

```python
import functools

import jax
import jax.numpy as jnp
import numpy as np
from jax import lax
from jax.experimental import pallas as pl
from jax.experimental.pallas import tpu as pltpu

F32 = jnp.float32
BF16 = jnp.bfloat16

D_MODEL = 2048
SEQ = 2048
CTX_LEN = 256
T_TOK = SEQ + CTX_LEN
GRID_W = 64
ROPE_THETA = 10000.0
EPS = 1e-6
DEPTH = 4
LOG2_E = 1.4426950408889634

LANES = 128
HEAD_DIM = 128
GQA_HEADS = 8
GQA_KV_HEADS = 2
GQA_GROUP = GQA_HEADS // GQA_KV_HEADS
GQA_WIDTH = GQA_HEADS * HEAD_DIM
GQA_KV_WIDTH = GQA_KV_HEADS * HEAD_DIM

MLA_HEADS = 8
MLA_NOPE = 128
MLA_ROPE = 64
MLA_QK = MLA_NOPE + MLA_ROPE
MLA_V = 128
MLA_Q_RANK = 512
MLA_KV_RANK = 512
MLA_WIDTH = MLA_HEADS * MLA_V
D_MIX = GQA_WIDTH + MLA_WIDTH
MLA_DK = MLA_NOPE + LANES
MLA_QK_WIDTH = MLA_HEADS * MLA_DK
V_EXT = 2 * HEAD_DIM

COL_Q = 0
COL_K = COL_Q + GQA_WIDTH
COL_V = COL_K + GQA_KV_WIDTH
COL_GG = COL_V + GQA_KV_WIDTH
COL_CQ = COL_GG + GQA_WIDTH
COL_CKV = COL_CQ + MLA_Q_RANK
COL_KR = COL_CKV + MLA_KV_RANK
COL_GM = COL_KR + MLA_ROPE
D_IN = COL_GM + MLA_WIDTH

TAB_COS_A, TAB_SIN_A, TAB_COS_M, TAB_SIN_M_LO, TAB_SIN_M_HI = range(5)
TAB_WIDTH = 5 * LANES

ADA_ROWS = 8
ADA_CTX_ROW = 4
ADA_TN = 1024

TM = 256
TQ_GQA = 512
TQ_MLA = 1024
N_BLK = T_TOK // TM
N_LAT_BLK = SEQ // TM
TM_OUT = 512
N_LAT_OUT = SEQ // TM_OUT
X_SLOTS = 3
W_CHUNK = 64
W_SLOTS = 8
MLA_HEADS_PER_STEP = 4

VMEM_LIMIT = 56 * 1024 * 1024

assert CTX_LEN == TM and SEQ % TQ_GQA == 0 and SEQ % TQ_MLA == 0


def _silu(x):
    return x / (1.0 + jnp.exp(-x))


def _rms(x):
    return x * lax.rsqrt(jnp.mean(x * x, axis=-1, keepdims=True) + EPS)


def _tab(tab_ref, col):
    return tab_ref[:, col * LANES:(col + 1) * LANES]


def _rot_a(x, tab_ref):
    return x * _tab(tab_ref, TAB_COS_A) + pltpu.roll(x, HEAD_DIM // 2, 1) * _tab(tab_ref, TAB_SIN_A)


def _rot_m(x, tab_ref, scale=None):
    half = MLA_ROPE // 2
    y = (x * _tab(tab_ref, TAB_COS_M)
         + pltpu.roll(x, LANES - half, 1) * _tab(tab_ref, TAB_SIN_M_LO)
         + pltpu.roll(x, half, 1) * _tab(tab_ref, TAB_SIN_M_HI))
    return y if scale is None else y * scale


def _mod_row(b, j):
    return jnp.where(j == N_LAT_BLK, ADA_CTX_ROW, b)


def _stream_block(xl_ref, xc_ref):
    return jnp.where(pl.program_id(1) == N_LAT_BLK, xc_ref[...], xl_ref[...])


def _ada_kernel(c_ref, w_ref, b_ref, o_ref):
    s = _silu(c_ref[...]).astype(BF16)
    w = w_ref[...].astype(BF16)
    o_ref[...] = jnp.dot(s, w, preferred_element_type=F32) + b_ref[...]


def _ada_rows(c_all, w_ada, b_ada):
    n_out = 3 * D_MODEL
    return pl.pallas_call(
        _ada_kernel,
        grid=(DEPTH, n_out // ADA_TN),
        in_specs=[
            pl.BlockSpec((ADA_ROWS, D_MODEL), lambda l, n: (0, 0)),
            pl.BlockSpec((None, D_MODEL, ADA_TN), lambda l, n: (l, 0, n)),
            pl.BlockSpec((None, 1, ADA_TN), lambda l, n: (l, 0, n)),
        ],
        out_specs=pl.BlockSpec((None, ADA_ROWS, ADA_TN), lambda l, n: (l, 0, n)),
        out_shape=jax.ShapeDtypeStruct((DEPTH, ADA_ROWS, n_out), F32),
        compiler_params=pltpu.CompilerParams(
            dimension_semantics=("arbitrary", "arbitrary"), vmem_limit_bytes=VMEM_LIMIT),
        name="ada_rows",
    )(c_all, w_ada, b_ada.reshape(DEPTH, 1, n_out))


def _store_bf16(dst_ref, rows, src_ref):
    dst_ref[rows, :] = src_ref[...].astype(BF16)


def _store_w_uq(dst_ref, rows, src_ref):
    low_half = lax.broadcasted_iota(jnp.int32, (W_CHUNK, LANES), 1) < MLA_ROPE
    for i in range(MLA_HEADS):
        c0 = i * MLA_QK
        tail = src_ref[:, c0 + MLA_QK - LANES:c0 + MLA_QK]
        rope = jnp.where(low_half, pltpu.roll(tail, MLA_ROPE, 1), 0.0)
        dst_ref[rows, i * MLA_DK:i * MLA_DK + MLA_NOPE] = src_ref[:, c0:c0 + MLA_NOPE].astype(BF16)
        dst_ref[rows, i * MLA_DK + MLA_NOPE:(i + 1) * MLA_DK] = rope.astype(BF16)


def _load_weight(w_hbm, layer, dst_ref, stage, sem, store=_store_bf16):
    n_chunks = w_hbm.shape[1] // W_CHUNK

    def rows(c):
        row0 = c * W_CHUNK
        return pl.ds(row0 if isinstance(c, int) else pl.multiple_of(row0, W_CHUNK), W_CHUNK)

    def chunk(c):
        slot = c % W_SLOTS
        return pltpu.make_async_copy(w_hbm.at[layer, rows(c), :], stage.at[slot], sem.at[slot])

    for c in range(W_SLOTS):
        chunk(c).start()

    def convert(c, carry):
        chunk(c).wait()
        store(dst_ref, rows(c), stage.at[c % W_SLOTS])
        pl.when(c + W_SLOTS < n_chunks)(lambda: chunk(c + W_SLOTS).start())
        return carry

    lax.fori_loop(0, n_chunks, convert, 0)


def _first_step():
    return jnp.logical_and(pl.program_id(0) == 0, pl.program_id(1) == 0)


def _in_proj_kernel(*refs, layer, paired):
    x_refs, refs = (refs[:2], refs[2:]) if paired else (refs[:1], refs[1:])
    (mod_ref, ng_ref, w_in_hbm, qg_ref, kg_ref, cqg_ref, ckvg_ref, w_uq_hbm, w_ukv_hbm, tab_ref,
     qa_ref, ka_ref, va_ref, qm_ref, km_ref, vm_ref, g_ref,
     w_in_ref, w_uq_ref, w_ukv_ref, stage, stage_uq, sem) = refs

    @pl.when(_first_step())
    def _():
        _load_weight(w_in_hbm, layer, w_in_ref, stage, sem)
        _load_weight(w_ukv_hbm, layer, w_ukv_ref, stage, sem)
        _load_weight(w_uq_hbm, layer, w_uq_ref, stage_uq, sem, _store_w_uq)

    x = _stream_block(*x_refs) if paired else x_refs[0][...]
    h = _rms(x) * (ng_ref[...] * (1.0 + mod_ref[1:2, :])) + mod_ref[0:1, :]
    hb = h.astype(BF16)
    proj = lambda r0, r1: lax.dot_general(hb, w_in_ref[r0:r1, :], (((1,), (1,)), ((), ())),
                                          preferred_element_type=F32)
    c = proj(COL_CQ, COL_KR)
    cq = (_rms(c[:, :MLA_Q_RANK]) * cqg_ref[...]).astype(BF16)
    ckv = (_rms(c[:, MLA_Q_RANK:]) * ckvg_ref[...]).astype(BF16)

    p = proj(COL_Q, COL_GG)
    q_gain = qg_ref[...] * (HEAD_DIM ** -0.5 * LOG2_E)
    for i in range(GQA_HEADS):
        c0 = COL_Q + i * HEAD_DIM
        qh = _rms(p[:, c0:c0 + HEAD_DIM]) * q_gain
        qa_ref[:, i * HEAD_DIM:(i + 1) * HEAD_DIM] = _rot_a(qh, tab_ref).astype(BF16)
    k_gain = kg_ref[...]
    for i in range(GQA_KV_HEADS):
        c0 = COL_K + i * HEAD_DIM
        kh = _rms(p[:, c0:c0 + HEAD_DIM]) * k_gain
        ka_ref[:, i * HEAD_DIM:(i + 1) * HEAD_DIM] = _rot_a(kh, tab_ref).astype(BF16)
    ones = jnp.ones((TM, HEAD_DIM), BF16)
    for i in range(GQA_KV_HEADS):
        c0 = COL_V + i * HEAD_DIM
        va_ref[:, i * V_EXT:i * V_EXT + HEAD_DIM] = p[:, c0:c0 + HEAD_DIM].astype(BF16)
        va_ref[:, i * V_EXT + HEAD_DIM:(i + 1) * V_EXT] = ones

    g_ref[:, :GQA_WIDTH] = _silu(proj(COL_GG, COL_CQ)).astype(BF16)
    g_ref[:, GQA_WIDTH:] = _silu(proj(COL_GM, D_IN)).astype(BF16)
    kr_in = proj(COL_KR, COL_KR + LANES)
    kr_in = jnp.where(lax.broadcasted_iota(jnp.int32, kr_in.shape, 1) < MLA_ROPE, kr_in, 0.0)

    qm = jnp.dot(cq, w_uq_ref[...], preferred_element_type=F32)
    m_scale = MLA_QK ** -0.5 * LOG2_E
    for i in range(MLA_HEADS):
        c0 = i * MLA_DK
        qm_ref[:, c0:c0 + MLA_NOPE] = (qm[:, c0:c0 + MLA_NOPE] * m_scale).astype(BF16)
        qm_ref[:, c0 + MLA_NOPE:c0 + MLA_DK] = _rot_m(
            qm[:, c0 + MLA_NOPE:c0 + MLA_DK], tab_ref, m_scale).astype(BF16)

    kvm = jnp.dot(ckv, w_ukv_ref[...], preferred_element_type=F32)
    kr = _rot_m(kr_in, tab_ref).astype(BF16)
    for i in range(MLA_HEADS):
        c0 = i * (MLA_NOPE + MLA_V)
        km_ref[:, i * MLA_DK:i * MLA_DK + MLA_NOPE] = kvm[:, c0:c0 + MLA_NOPE].astype(BF16)
        km_ref[:, i * MLA_DK + MLA_NOPE:(i + 1) * MLA_DK] = kr
        vm_ref[:, i * V_EXT:i * V_EXT + MLA_V] = kvm[:, c0 + MLA_NOPE:c0 + MLA_NOPE + MLA_V].astype(BF16)
        vm_ref[:, i * V_EXT + MLA_V:(i + 1) * V_EXT] = ones


def _in_proj(layer, stream, mod, params, tab):
    n_b = stream[0].shape[0]
    paired = len(stream) == 2
    tok = lambda width: pl.BlockSpec((None, TM, width), lambda b, j: (b, j, 0))
    stream_specs = [
        pl.BlockSpec((None, TM, D_MODEL), lambda b, j: (b, jnp.minimum(j, N_LAT_BLK - 1), 0)),
        pl.BlockSpec((None, CTX_LEN, D_MODEL), lambda b, j: (b, 0, 0))] if paired else [tok(D_MODEL)]
    lay = lambda *shape: pl.BlockSpec((None,) + shape, lambda b, j: (layer,) + (0,) * len(shape),
                                      pipeline_mode=pl.Buffered(1))
    f32_hbm = pl.BlockSpec(memory_space=pl.ANY)
    widths = (GQA_WIDTH, GQA_KV_WIDTH, GQA_KV_HEADS * V_EXT,
              MLA_QK_WIDTH, MLA_QK_WIDTH, MLA_HEADS * V_EXT, D_MIX)
    return pl.pallas_call(
        functools.partial(_in_proj_kernel, layer=layer, paired=paired),
        grid=(n_b, N_BLK),
        in_specs=stream_specs + [
            pl.BlockSpec((None, None, 3, D_MODEL), lambda b, j: (layer, _mod_row(b, j), 0, 0)),
            lay(1, D_MODEL),
            f32_hbm,
            lay(1, HEAD_DIM), lay(1, HEAD_DIM), lay(1, MLA_Q_RANK), lay(1, MLA_KV_RANK),
            f32_hbm,
            f32_hbm,
            pl.BlockSpec((TM, TAB_WIDTH), lambda b, j: (j, 0)),
        ],
        out_specs=[tok(w) for w in widths],
        out_shape=[jax.ShapeDtypeStruct((n_b, T_TOK, w), BF16) for w in widths],
        scratch_shapes=[pltpu.VMEM((D_IN, D_MODEL), BF16),
                        pltpu.VMEM((MLA_Q_RANK, MLA_QK_WIDTH), BF16),
                        pltpu.VMEM((MLA_KV_RANK, MLA_HEADS * (MLA_NOPE + MLA_V)), BF16),
                        pltpu.VMEM((W_SLOTS, W_CHUNK, D_MODEL), F32),
                        pltpu.VMEM((W_SLOTS, W_CHUNK, MLA_HEADS * MLA_QK), F32),
                        pltpu.SemaphoreType.DMA((W_SLOTS,))],
        compiler_params=pltpu.CompilerParams(
            dimension_semantics=("arbitrary", "arbitrary"), vmem_limit_bytes=VMEM_LIMIT),
        name="in_proj",
    )(*stream, mod, *params, tab)


def _attend(q_ref, k_ref, v_ref, g_ref, o_ref, *, kv_heads, group, dk, dv):
    for p in range(kv_heads):
        k = k_ref[:, p * dk:(p + 1) * dk]
        v = v_ref[:, p * V_EXT:(p + 1) * V_EXT]
        for r0 in range(0, q_ref.shape[0], TM):
            for i in range(group):
                hd = p * group + i
                q = q_ref[r0:r0 + TM, hd * dk:(hd + 1) * dk]
                s = lax.dot_general(q, k, (((1,), (1,)), ((), ())), preferred_element_type=F32)
                e = jnp.exp2(s - jnp.max(s, axis=-1, keepdims=True))
                o = jnp.dot(e.astype(BF16), v, preferred_element_type=F32)
                gate = g_ref[r0:r0 + TM, hd * dv:(hd + 1) * dv].astype(F32)
                o_ref[r0:r0 + TM, hd * dv:(hd + 1) * dv] = (
                    o[:, :dv] * (gate / o[:, dv:])).astype(BF16)


_GQA_CFG = dict(kv_heads=GQA_KV_HEADS, group=GQA_GROUP, dk=HEAD_DIM, dv=HEAD_DIM)


def _attn_gqa(q, k, v, g):
    n_b = q.shape[0]
    return pl.pallas_call(
        functools.partial(_attend, **_GQA_CFG),
        grid=(n_b, SEQ // TQ_GQA),
        in_specs=[
            pl.BlockSpec((None, TQ_GQA, GQA_WIDTH), lambda b, j: (b, j, 0)),
            pl.BlockSpec((None, T_TOK, GQA_KV_WIDTH), lambda b, j: (b, 0, 0)),
            pl.BlockSpec((None, T_TOK, GQA_KV_HEADS * V_EXT), lambda b, j: (b, 0, 0)),
            pl.BlockSpec((None, TQ_GQA, GQA_WIDTH), lambda b, j: (b, j, 0)),
        ],
        out_specs=pl.BlockSpec((None, TQ_GQA, GQA_WIDTH), lambda b, j: (b, j, 0)),
        out_shape=jax.ShapeDtypeStruct((n_b, SEQ, GQA_WIDTH), BF16),
        compiler_params=pltpu.CompilerParams(
            dimension_semantics=("arbitrary", "arbitrary"), vmem_limit_bytes=VMEM_LIMIT),
        name="attn_gqa",
    )(q, k, v, g)


def _attn_mla(q, k, v, g):
    n_b = q.shape[0]
    hps = MLA_HEADS_PER_STEP
    g_col0 = GQA_WIDTH // (hps * MLA_V)
    return pl.pallas_call(
        functools.partial(_attend, kv_heads=hps, group=1, dk=MLA_DK, dv=MLA_V),
        grid=(n_b, MLA_HEADS // hps, SEQ // TQ_MLA),
        in_specs=[
            pl.BlockSpec((None, TQ_MLA, hps * MLA_DK), lambda b, hg, j: (b, j, hg)),
            pl.BlockSpec((None, T_TOK, hps * MLA_DK), lambda b, hg, j: (b, 0, hg)),
            pl.BlockSpec((None, T_TOK, hps * V_EXT), lambda b, hg, j: (b, 0, hg)),
            pl.BlockSpec((None, TQ_MLA, hps * MLA_V), lambda b, hg, j: (b, j, g_col0 + hg)),
        ],
        out_specs=pl.BlockSpec((None, TQ_MLA, hps * MLA_V), lambda b, hg, j: (b, j, hg)),
        out_shape=jax.ShapeDtypeStruct((n_b, SEQ, MLA_WIDTH), BF16),
        compiler_params=pltpu.CompilerParams(
            dimension_semantics=("arbitrary", "arbitrary", "arbitrary"), vmem_limit_bytes=VMEM_LIMIT),
        name="attn_mla",
    )(q, k, v, g)


def _attn_ctx_kernel(qa_ref, ka_ref, va_ref, qm_ref, km_ref, vm_ref, g_ref, oa_ref, om_ref):
    _attend(qa_ref, ka_ref, va_ref, g_ref.at[:, :GQA_WIDTH], oa_ref, **_GQA_CFG)
    _attend(qm_ref, km_ref, vm_ref, g_ref.at[:, GQA_WIDTH:], om_ref,
            kv_heads=MLA_HEADS, group=1, dk=MLA_DK, dv=MLA_V)


def _attn_ctx(qa, ka, va, qm, km, vm, g):
    n_b = qa.shape[0]
    ctx_rows = lambda width: pl.BlockSpec((None, CTX_LEN, width), lambda b: (b, SEQ // CTX_LEN, 0))
    out = lambda width: pl.BlockSpec((None, CTX_LEN, width), lambda b: (b, 0, 0))
    return pl.pallas_call(
        _attn_ctx_kernel,
        grid=(n_b,),
        in_specs=[ctx_rows(GQA_WIDTH), ctx_rows(GQA_KV_WIDTH), ctx_rows(GQA_KV_HEADS * V_EXT),
                  ctx_rows(MLA_QK_WIDTH), ctx_rows(MLA_QK_WIDTH), ctx_rows(MLA_HEADS * V_EXT),
                  ctx_rows(D_MIX)],
        out_specs=[out(GQA_WIDTH), out(MLA_WIDTH)],
        out_shape=[jax.ShapeDtypeStruct((n_b, CTX_LEN, GQA_WIDTH), BF16),
                   jax.ShapeDtypeStruct((n_b, CTX_LEN, MLA_WIDTH), BF16)],
        compiler_params=pltpu.CompilerParams(
            dimension_semantics=("arbitrary",), vmem_limit_bytes=VMEM_LIMIT),
        name="attn_ctx",
    )(qa, ka, va, qm, km, vm, g)


def _out_proj_kernel(*refs, layer, last, n_blk):
    refs, (w_ref, xbuf, abuf, mbuf, stage, sem, wsem) = refs[:-7], refs[-7:]
    if last:
        ya_hbm, ym_hbm, w_hbm, x_hbm, mod_ref, fg_ref, o_ref = refs
    else:
        ya_hbm, ym_hbm, yca_ref, ycm_ref, w_hbm, x_hbm, xc_ref, mod_ref, o_ref = refs

    def residual(a_ref, m_ref, x_ref):
        y = jnp.concatenate([a_ref[...], m_ref[...]], axis=-1)
        return x_ref[...] + mod_ref[2:3, :] * jnp.dot(y, w_ref[...], preferred_element_type=F32)

    j = pl.program_id(1)
    step = pl.program_id(0) * n_blk + j
    n_steps = pl.num_programs(0) * n_blk
    rings = ((x_hbm, xbuf), (ya_hbm, abuf), (ym_hbm, mbuf))

    def fetch(s):
        s = jnp.asarray(s, jnp.int32)
        row0 = pl.multiple_of((s % n_blk) * TM_OUT, TM_OUT)
        slot = s % X_SLOTS
        return [pltpu.make_async_copy(hbm.at[s // n_blk, pl.ds(row0, TM_OUT), :], buf.at[slot],
                                      sem.at[i, slot]) for i, (hbm, buf) in enumerate(rings)]

    def start_fetch(s):
        s = jnp.asarray(s, jnp.int32)

        @pl.when(jnp.logical_and(s < n_steps, s % n_blk < N_LAT_OUT))
        def _():
            for copy in fetch(s):
                copy.start()

    @pl.when(step == 0)
    def _():
        for s in range(X_SLOTS - 1):
            start_fetch(s)
        _load_weight(w_hbm, layer, w_ref, stage, wsem)

    start_fetch(step + (X_SLOTS - 1))

    @pl.when(j < N_LAT_OUT)
    def _():
        for copy in fetch(step):
            copy.wait()
        slot = step % X_SLOTS
        x_new = residual(abuf.at[slot], mbuf.at[slot], xbuf.at[slot])
        o_ref[...] = _rms(x_new) * fg_ref[...] if last else x_new

    if not last:
        @pl.when(j == N_LAT_OUT)
        def _():
            o_ref[:CTX_LEN, :] = residual(yca_ref, ycm_ref, xc_ref)
            o_ref[CTX_LEN:, :] = jnp.zeros((TM_OUT - CTX_LEN, D_MODEL), F32)


def _out_proj(layer, y_lat, y_ctx, w_out, stream, mod, final_g):
    n_b = y_lat[0].shape[0]
    last = y_ctx is None
    n_blk = N_LAT_OUT if last else N_LAT_OUT + 1
    ctx_blk = 0 if len(stream) == 2 else SEQ // CTX_LEN
    ctx = lambda width, blk: pl.BlockSpec((None, CTX_LEN, width), lambda b, j: (b, blk, 0))
    hbm = pl.BlockSpec(memory_space=pl.ANY)
    mod_spec = pl.BlockSpec((None, None, 3, D_MODEL),
                            lambda b, j: (layer, jnp.where(j == N_LAT_OUT, ADA_CTX_ROW, b), 0, 0))
    if last:
        in_specs = [hbm, hbm, hbm, hbm, mod_spec, pl.BlockSpec((1, D_MODEL), lambda b, j: (0, 0))]
        args = (*y_lat, w_out, stream[0], mod, final_g.reshape(1, D_MODEL))
    else:
        in_specs = [hbm, hbm, ctx(GQA_WIDTH, 0), ctx(MLA_WIDTH, 0), hbm,
                    hbm, ctx(D_MODEL, ctx_blk), mod_spec]
        args = (*y_lat, *y_ctx, w_out, stream[0], stream[-1], mod)
    return pl.pallas_call(
        functools.partial(_out_proj_kernel, layer=layer, last=last, n_blk=n_blk),
        grid=(n_b, n_blk),
        in_specs=in_specs,
        out_specs=pl.BlockSpec((None, TM_OUT, D_MODEL), lambda b, j: (b, j, 0)),
        out_shape=jax.ShapeDtypeStruct((n_b, n_blk * TM_OUT, D_MODEL), F32),
        scratch_shapes=[pltpu.VMEM((D_MIX, D_MODEL), BF16),
                        pltpu.VMEM((X_SLOTS, TM_OUT, D_MODEL), F32),
                        pltpu.VMEM((X_SLOTS, TM_OUT, GQA_WIDTH), BF16),
                        pltpu.VMEM((X_SLOTS, TM_OUT, MLA_WIDTH), BF16),
                        pltpu.VMEM((W_SLOTS, W_CHUNK, D_MODEL), F32),
                        pltpu.SemaphoreType.DMA((3, X_SLOTS)),
                        pltpu.SemaphoreType.DMA((W_SLOTS,))],
        compiler_params=pltpu.CompilerParams(
            dimension_semantics=("arbitrary", "arbitrary"), vmem_limit_bytes=VMEM_LIMIT),
        name="out_proj",
    )(*args)


def _rope_table():
    f32 = np.float32
    rows = SEQ // GRID_W
    row = np.repeat(np.arange(rows, dtype=f32), GRID_W)
    col = np.tile(np.arange(GRID_W, dtype=f32), rows)

    def cos_sin(rot_dim):
        n_freq = rot_dim // 4
        inv = (f32(ROPE_THETA) ** (-np.arange(n_freq, dtype=f32) / f32(n_freq))).astype(f32)
        ang = np.concatenate([row[:, None] * inv[None], col[:, None] * inv[None]], axis=-1)
        return np.cos(ang).astype(f32), np.sin(ang).astype(f32)

    cos, sin = cos_sin(HEAD_DIM)
    cos_m, sin_m = cos_sin(MLA_ROPE)
    z32 = np.zeros_like(sin_m)
    z64 = np.zeros_like(sin)
    lat = np.concatenate([
        cos, cos,
        -sin, sin,
        cos_m, cos_m, z64,
        -sin_m, z32, z64,
        z32, sin_m, z64,
    ], axis=-1)
    ctx_row = np.concatenate([np.ones(LANES, f32), np.zeros(LANES, f32),
                              np.ones(LANES, f32), np.zeros(2 * LANES, f32)])
    return jnp.asarray(np.concatenate([lat, np.broadcast_to(ctx_row, (CTX_LEN, TAB_WIDTH))], axis=0))


def kernel(x, c, ctx, c_ctx, w_ada, b_ada, norm_g, w_in, q_gain, k_gain, cq_gain, ckv_gain,
           w_uq, w_ukv, w_out, final_g):
    n_b = x.shape[0]
    assert x.shape == (n_b, SEQ, D_MODEL) and ctx.shape == (n_b, CTX_LEN, D_MODEL)
    assert n_b == ADA_CTX_ROW and w_ada.shape[0] == DEPTH and w_in.shape[-1] == D_IN

    c_all = jnp.concatenate(
        [c, c_ctx[None], jnp.zeros((ADA_ROWS - n_b - 1, D_MODEL), F32)], axis=0)
    mod = _ada_rows(c_all, w_ada, b_ada).reshape(DEPTH, ADA_ROWS, 3, D_MODEL)

    row3 = lambda a: a.reshape(DEPTH, 1, a.shape[-1])
    params = (row3(norm_g), jnp.swapaxes(w_in, 1, 2), row3(q_gain), row3(k_gain), row3(cq_gain),
              row3(ckv_gain), w_uq, w_ukv)
    tab = _rope_table()

    stream = (x, ctx)
    for layer in range(DEPTH):
        last = layer == DEPTH - 1
        qa, ka, va, qm, km, vm, g = _in_proj(layer, stream, mod, params, tab)
        y_lat = (_attn_gqa(qa, ka, va, g), _attn_mla(qm, km, vm, g))
        y_ctx = None if last else _attn_ctx(qa, ka, va, qm, km, vm, g)
        stream = (_out_proj(layer, y_lat, y_ctx, w_out, stream, mod, final_g),)
    return stream[0]
```

```python
import functools

import jax
import jax.numpy as jnp
import numpy as np
from jax import lax
from jax.experimental import pallas as pl
from jax.experimental.pallas import tpu as pltpu

F32 = jnp.float32
BF16 = jnp.bfloat16

D_MODEL = 2048
SEQ = 2048
CTX_LEN = 256
T_TOK = SEQ + CTX_LEN
GRID_W = 64
ROPE_THETA = 10000.0
EPS = 1e-6
DEPTH = 4
LOG2_E = 1.4426950408889634

LANES = 128
HEAD_DIM = 128
GQA_HEADS = 8
GQA_KV_HEADS = 2
GQA_GROUP = GQA_HEADS // GQA_KV_HEADS
GQA_WIDTH = GQA_HEADS * HEAD_DIM
GQA_KV_WIDTH = GQA_KV_HEADS * HEAD_DIM

MLA_HEADS = 8
MLA_NOPE = 128
MLA_ROPE = 64
MLA_QK = MLA_NOPE + MLA_ROPE
MLA_V = 128
MLA_Q_RANK = 512
MLA_KV_RANK = 512
MLA_WIDTH = MLA_HEADS * MLA_V
D_MIX = GQA_WIDTH + MLA_WIDTH
MLA_DK = MLA_NOPE + LANES
MLA_QK_WIDTH = MLA_HEADS * MLA_DK
V_EXT = 2 * HEAD_DIM

COL_Q = 0
COL_K = COL_Q + GQA_WIDTH
COL_V = COL_K + GQA_KV_WIDTH
COL_GG = COL_V + GQA_KV_WIDTH
COL_CQ = COL_GG + GQA_WIDTH
COL_CKV = COL_CQ + MLA_Q_RANK
COL_KR = COL_CKV + MLA_KV_RANK
COL_GM = COL_KR + MLA_ROPE
D_IN = COL_GM + MLA_WIDTH

TAB_COS_A, TAB_SIN_A, TAB_COS_M, TAB_SIN_M_LO, TAB_SIN_M_HI = range(5)
TAB_WIDTH = 5 * LANES

ADA_ROWS = 8
ADA_CTX_ROW = 4
ADA_TN = 1024

TM = 256
TQ_GQA = 512
TQ_MLA = 1024
N_BLK = T_TOK // TM
N_LAT_BLK = SEQ // TM
TM_OUT = 512
N_LAT_OUT = SEQ // TM_OUT
X_SLOTS = 3
W_CHUNK = 64
W_SLOTS = 8
MLA_HEADS_PER_STEP = 4

VMEM_LIMIT = 56 * 1024 * 1024

assert CTX_LEN == TM and SEQ % TQ_GQA == 0 and SEQ % TQ_MLA == 0


def _silu(x):
    return x / (1.0 + jnp.exp(-x))


def _rms(x):
    return x * lax.rsqrt(jnp.mean(x * x, axis=-1, keepdims=True) + EPS)


def _tab(tab_ref, col):
    return tab_ref[:, col * LANES:(col + 1) * LANES]


def _rot_a(x, tab_ref):
    return x * _tab(tab_ref, TAB_COS_A) + pltpu.roll(x, HEAD_DIM // 2, 1) * _tab(tab_ref, TAB_SIN_A)


def _rot_m(x, tab_ref, scale=None):
    half = MLA_ROPE // 2
    y = (x * _tab(tab_ref, TAB_COS_M)
         + pltpu.roll(x, LANES - half, 1) * _tab(tab_ref, TAB_SIN_M_LO)
         + pltpu.roll(x, half, 1) * _tab(tab_ref, TAB_SIN_M_HI))
    return y if scale is None else y * scale


def _mod_row(b, j):
    return jnp.where(j == N_LAT_BLK, ADA_CTX_ROW, b)


def _stream_block(xl_ref, xc_ref):
    return jnp.where(pl.program_id(1) == N_LAT_BLK, xc_ref[...], xl_ref[...])


def _ada_kernel(c_ref, w_ref, b_ref, o_ref):
    s = _silu(c_ref[...]).astype(BF16)
    w = w_ref[...].astype(BF16)
    o_ref[...] = jnp.dot(s, w, preferred_element_type=F32) + b_ref[...]


def _ada_rows(c_all, w_ada, b_ada):
    n_out = 3 * D_MODEL
    return pl.pallas_call(
        _ada_kernel,
        grid=(DEPTH, n_out // ADA_TN),
        in_specs=[
            pl.BlockSpec((ADA_ROWS, D_MODEL), lambda l, n: (0, 0)),
            pl.BlockSpec((None, D_MODEL, ADA_TN), lambda l, n: (l, 0, n)),
            pl.BlockSpec((None, 1, ADA_TN), lambda l, n: (l, 0, n)),
        ],
        out_specs=pl.BlockSpec((None, ADA_ROWS, ADA_TN), lambda l, n: (l, 0, n)),
        out_shape=jax.ShapeDtypeStruct((DEPTH, ADA_ROWS, n_out), F32),
        compiler_params=pltpu.CompilerParams(
            dimension_semantics=("arbitrary", "arbitrary"), vmem_limit_bytes=VMEM_LIMIT),
        name="ada_rows",
    )(c_all, w_ada, b_ada.reshape(DEPTH, 1, n_out))


def _store_bf16(dst_ref, rows, src_ref):
    dst_ref[rows, :] = src_ref[...].astype(BF16)


def _store_w_uq(dst_ref, rows, src_ref):
    low_half = lax.broadcasted_iota(jnp.int32, (W_CHUNK, LANES), 1) < MLA_ROPE
    for i in range(MLA_HEADS):
        c0 = i * MLA_QK
        tail = src_ref[:, c0 + MLA_QK - LANES:c0 + MLA_QK]
        rope = jnp.where(low_half, pltpu.roll(tail, MLA_ROPE, 1), 0.0)
        dst_ref[rows, i * MLA_DK:i * MLA_DK + MLA_NOPE] = src_ref[:, c0:c0 + MLA_NOPE].astype(BF16)
        dst_ref[rows, i * MLA_DK + MLA_NOPE:(i + 1) * MLA_DK] = rope.astype(BF16)


def _load_weight(w_hbm, layer, dst_ref, stage, sem, store=_store_bf16):
    n_chunks = w_hbm.shape[1] // W_CHUNK

    def rows(c):
        row0 = c * W_CHUNK
        return pl.ds(row0 if isinstance(c, int) else pl.multiple_of(row0, W_CHUNK), W_CHUNK)

    def chunk(c):
        slot = c % W_SLOTS
        return pltpu.make_async_copy(w_hbm.at[layer, rows(c), :], stage.at[slot], sem.at[slot])

    for c in range(W_SLOTS):
        chunk(c).start()

    def convert(c, carry):
        chunk(c).wait()
        store(dst_ref, rows(c), stage.at[c % W_SLOTS])
        pl.when(c + W_SLOTS < n_chunks)(lambda: chunk(c + W_SLOTS).start())
        return carry

    lax.fori_loop(0, n_chunks, convert, 0)


def _first_step():
    return jnp.logical_and(pl.program_id(0) == 0, pl.program_id(1) == 0)


def _in_proj_kernel(*refs, layer, paired, last):
    x_refs, refs = (refs[:2], refs[2:]) if paired else (refs[:1], refs[1:])
    (mod_ref, ng_ref, w_in_hbm, qg_ref, kg_ref, cqg_ref, ckvg_ref, w_uq_hbm, w_ukv_hbm, tab_ref,
     qa_ref, ka_ref, va_ref, qm_ref, km_ref, vm_ref, g_ref,
     w_in_ref, w_uq_ref, w_ukv_ref, stage, stage_uq, sem) = refs

    @pl.when(_first_step())
    def _():
        _load_weight(w_in_hbm, layer, w_in_ref, stage, sem)
        _load_weight(w_ukv_hbm, layer, w_ukv_ref, stage, sem)
        _load_weight(w_uq_hbm, layer, w_uq_ref, stage_uq, sem, _store_w_uq)

    def project(keys_only):
        x = _stream_block(*x_refs) if paired else x_refs[0][...]
        h = _rms(x) * (ng_ref[...] * (1.0 + mod_ref[1:2, :])) + mod_ref[0:1, :]
        hb = h.astype(BF16)
        proj = lambda r0, r1: lax.dot_general(hb, w_in_ref[r0:r1, :], (((1,), (1,)), ((), ())),
                                              preferred_element_type=F32)
        c0 = COL_CKV if keys_only else COL_CQ
        c = proj(c0, COL_KR)
        ckv = (_rms(c[:, COL_CKV - c0:]) * ckvg_ref[...]).astype(BF16)

        p0 = COL_K if keys_only else COL_Q
        p = proj(p0, COL_GG)
        if keys_only:
            qa_ref[...] = jnp.zeros(qa_ref.shape, BF16)
        else:
            q_gain = qg_ref[...] * (HEAD_DIM ** -0.5 * LOG2_E)
            for i in range(GQA_HEADS):
                q0 = COL_Q + i * HEAD_DIM
                qh = _rms(p[:, q0:q0 + HEAD_DIM]) * q_gain
                qa_ref[:, i * HEAD_DIM:(i + 1) * HEAD_DIM] = _rot_a(qh, tab_ref).astype(BF16)
        k_gain = kg_ref[...]
        for i in range(GQA_KV_HEADS):
            k0 = COL_K - p0 + i * HEAD_DIM
            kh = _rms(p[:, k0:k0 + HEAD_DIM]) * k_gain
            ka_ref[:, i * HEAD_DIM:(i + 1) * HEAD_DIM] = _rot_a(kh, tab_ref).astype(BF16)
        ones = jnp.ones((TM, HEAD_DIM), BF16)
        for i in range(GQA_KV_HEADS):
            v0 = COL_V - p0 + i * HEAD_DIM
            va_ref[:, i * V_EXT:i * V_EXT + HEAD_DIM] = p[:, v0:v0 + HEAD_DIM].astype(BF16)
            va_ref[:, i * V_EXT + HEAD_DIM:(i + 1) * V_EXT] = ones

        if keys_only:
            g_ref[...] = jnp.zeros(g_ref.shape, BF16)
        else:
            g_ref[:, :GQA_WIDTH] = _silu(proj(COL_GG, COL_CQ)).astype(BF16)
            g_ref[:, GQA_WIDTH:] = _silu(proj(COL_GM, D_IN)).astype(BF16)
        kr_in = proj(COL_KR, COL_KR + LANES)
        kr_in = jnp.where(lax.broadcasted_iota(jnp.int32, kr_in.shape, 1) < MLA_ROPE, kr_in, 0.0)

        if keys_only:
            qm_ref[...] = jnp.zeros(qm_ref.shape, BF16)
        else:
            cq = (_rms(c[:, :MLA_Q_RANK]) * cqg_ref[...]).astype(BF16)
            qm = jnp.dot(cq, w_uq_ref[...], preferred_element_type=F32)
            m_scale = MLA_QK ** -0.5 * LOG2_E
            for i in range(MLA_HEADS):
                m0 = i * MLA_DK
                qm_ref[:, m0:m0 + MLA_NOPE] = (qm[:, m0:m0 + MLA_NOPE] * m_scale).astype(BF16)
                qm_ref[:, m0 + MLA_NOPE:m0 + MLA_DK] = _rot_m(
                    qm[:, m0 + MLA_NOPE:m0 + MLA_DK], tab_ref, m_scale).astype(BF16)

        kvm = jnp.dot(ckv, w_ukv_ref[...], preferred_element_type=F32)
        kr = _rot_m(kr_in, tab_ref).astype(BF16)
        for i in range(MLA_HEADS):
            u0 = i * (MLA_NOPE + MLA_V)
            km_ref[:, i * MLA_DK:i * MLA_DK + MLA_NOPE] = kvm[:, u0:u0 + MLA_NOPE].astype(BF16)
            km_ref[:, i * MLA_DK + MLA_NOPE:(i + 1) * MLA_DK] = kr
            vm_ref[:, i * V_EXT:i * V_EXT + MLA_V] = (
                kvm[:, u0 + MLA_NOPE:u0 + MLA_NOPE + MLA_V].astype(BF16))
            vm_ref[:, i * V_EXT + MLA_V:(i + 1) * V_EXT] = ones

    if last:
        is_ctx = pl.program_id(1) == N_LAT_BLK
        pl.when(jnp.logical_not(is_ctx))(lambda: project(False))
        pl.when(is_ctx)(lambda: project(True))
    else:
        project(False)


def _in_proj(layer, stream, mod, params, tab):
    n_b = stream[0].shape[0]
    paired = len(stream) == 2
    tok = lambda width: pl.BlockSpec((None, TM, width), lambda b, j: (b, j, 0))
    stream_specs = [
        pl.BlockSpec((None, TM, D_MODEL), lambda b, j: (b, jnp.minimum(j, N_LAT_BLK - 1), 0)),
        pl.BlockSpec((None, CTX_LEN, D_MODEL), lambda b, j: (b, 0, 0))] if paired else [tok(D_MODEL)]
    lay = lambda *shape: pl.BlockSpec((None,) + shape, lambda b, j: (layer,) + (0,) * len(shape),
                                      pipeline_mode=pl.Buffered(1))
    f32_hbm = pl.BlockSpec(memory_space=pl.ANY)
    widths = (GQA_WIDTH, GQA_KV_WIDTH, GQA_KV_HEADS * V_EXT,
              MLA_QK_WIDTH, MLA_QK_WIDTH, MLA_HEADS * V_EXT, D_MIX)
    return pl.pallas_call(
        functools.partial(_in_proj_kernel, layer=layer, paired=paired, last=layer == DEPTH - 1),
        grid=(n_b, N_BLK),
        in_specs=stream_specs + [
            pl.BlockSpec((None, None, 3, D_MODEL), lambda b, j: (layer, _mod_row(b, j), 0, 0)),
            lay(1, D_MODEL),
            f32_hbm,
            lay(1, HEAD_DIM), lay(1, HEAD_DIM), lay(1, MLA_Q_RANK), lay(1, MLA_KV_RANK),
            f32_hbm,
            f32_hbm,
            pl.BlockSpec((TM, TAB_WIDTH), lambda b, j: (j, 0)),
        ],
        out_specs=[tok(w) for w in widths],
        out_shape=[jax.ShapeDtypeStruct((n_b, T_TOK, w), BF16) for w in widths],
        scratch_shapes=[pltpu.VMEM((D_IN, D_MODEL), BF16),
                        pltpu.VMEM((MLA_Q_RANK, MLA_QK_WIDTH), BF16),
                        pltpu.VMEM((MLA_KV_RANK, MLA_HEADS * (MLA_NOPE + MLA_V)), BF16),
                        pltpu.VMEM((W_SLOTS, W_CHUNK, D_MODEL), F32),
                        pltpu.VMEM((W_SLOTS, W_CHUNK, MLA_HEADS * MLA_QK), F32),
                        pltpu.SemaphoreType.DMA((W_SLOTS,))],
        compiler_params=pltpu.CompilerParams(
            dimension_semantics=("arbitrary", "arbitrary"), vmem_limit_bytes=VMEM_LIMIT),
        name="in_proj",
    )(*stream, mod, *params, tab)


def _attend(q_ref, k_ref, v_ref, g_ref, o_ref, *, kv_heads, group, dk, dv):
    for p in range(kv_heads):
        k = k_ref[:, p * dk:(p + 1) * dk]
        v = v_ref[:, p * V_EXT:(p + 1) * V_EXT]
        for r0 in range(0, q_ref.shape[0], TM):
            for i in range(group):
                hd = p * group + i
                q = q_ref[r0:r0 + TM, hd * dk:(hd + 1) * dk]
                s = lax.dot_general(q, k, (((1,), (1,)), ((), ())), preferred_element_type=F32)
                e = jnp.exp2(s - jnp.max(s, axis=-1, keepdims=True))
                o = jnp.dot(e.astype(BF16), v, preferred_element_type=F32)
                gate = g_ref[r0:r0 + TM, hd * dv:(hd + 1) * dv].astype(F32)
                o_ref[r0:r0 + TM, hd * dv:(hd + 1) * dv] = (
                    o[:, :dv] * (gate / o[:, dv:])).astype(BF16)


_GQA_CFG = dict(kv_heads=GQA_KV_HEADS, group=GQA_GROUP, dk=HEAD_DIM, dv=HEAD_DIM)


def _attn_gqa(q, k, v, g):
    n_b = q.shape[0]
    return pl.pallas_call(
        functools.partial(_attend, **_GQA_CFG),
        grid=(n_b, SEQ // TQ_GQA),
        in_specs=[
            pl.BlockSpec((None, TQ_GQA, GQA_WIDTH), lambda b, j: (b, j, 0)),
            pl.BlockSpec((None, T_TOK, GQA_KV_WIDTH), lambda b, j: (b, 0, 0)),
            pl.BlockSpec((None, T_TOK, GQA_KV_HEADS * V_EXT), lambda b, j: (b, 0, 0)),
            pl.BlockSpec((None, TQ_GQA, GQA_WIDTH), lambda b, j: (b, j, 0)),
        ],
        out_specs=pl.BlockSpec((None, TQ_GQA, GQA_WIDTH), lambda b, j: (b, j, 0)),
        out_shape=jax.ShapeDtypeStruct((n_b, SEQ, GQA_WIDTH), BF16),
        compiler_params=pltpu.CompilerParams(
            dimension_semantics=("arbitrary", "arbitrary"), vmem_limit_bytes=VMEM_LIMIT),
        name="attn_gqa",
    )(q, k, v, g)


def _attn_mla(q, k, v, g):
    n_b = q.shape[0]
    hps = MLA_HEADS_PER_STEP
    g_col0 = GQA_WIDTH // (hps * MLA_V)
    return pl.pallas_call(
        functools.partial(_attend, kv_heads=hps, group=1, dk=MLA_DK, dv=MLA_V),
        grid=(n_b, MLA_HEADS // hps, SEQ // TQ_MLA),
        in_specs=[
            pl.BlockSpec((None, TQ_MLA, hps * MLA_DK), lambda b, hg, j: (b, j, hg)),
            pl.BlockSpec((None, T_TOK, hps * MLA_DK), lambda b, hg, j: (b, 0, hg)),
            pl.BlockSpec((None, T_TOK, hps * V_EXT), lambda b, hg, j: (b, 0, hg)),
            pl.BlockSpec((None, TQ_MLA, hps * MLA_V), lambda b, hg, j: (b, j, g_col0 + hg)),
        ],
        out_specs=pl.BlockSpec((None, TQ_MLA, hps * MLA_V), lambda b, hg, j: (b, j, hg)),
        out_shape=jax.ShapeDtypeStruct((n_b, SEQ, MLA_WIDTH), BF16),
        compiler_params=pltpu.CompilerParams(
            dimension_semantics=("arbitrary", "arbitrary", "arbitrary"), vmem_limit_bytes=VMEM_LIMIT),
        name="attn_mla",
    )(q, k, v, g)


def _attn_ctx_kernel(qa_ref, ka_ref, va_ref, qm_ref, km_ref, vm_ref, g_ref, oa_ref, om_ref):
    _attend(qa_ref, ka_ref, va_ref, g_ref.at[:, :GQA_WIDTH], oa_ref, **_GQA_CFG)
    _attend(qm_ref, km_ref, vm_ref, g_ref.at[:, GQA_WIDTH:], om_ref,
            kv_heads=MLA_HEADS, group=1, dk=MLA_DK, dv=MLA_V)


def _attn_ctx(qa, ka, va, qm, km, vm, g):
    n_b = qa.shape[0]
    ctx_rows = lambda width: pl.BlockSpec((None, CTX_LEN, width), lambda b: (b, SEQ // CTX_LEN, 0))
    out = lambda width: pl.BlockSpec((None, CTX_LEN, width), lambda b: (b, 0, 0))
    return pl.pallas_call(
        _attn_ctx_kernel,
        grid=(n_b,),
        in_specs=[ctx_rows(GQA_WIDTH), ctx_rows(GQA_KV_WIDTH), ctx_rows(GQA_KV_HEADS * V_EXT),
                  ctx_rows(MLA_QK_WIDTH), ctx_rows(MLA_QK_WIDTH), ctx_rows(MLA_HEADS * V_EXT),
                  ctx_rows(D_MIX)],
        out_specs=[out(GQA_WIDTH), out(MLA_WIDTH)],
        out_shape=[jax.ShapeDtypeStruct((n_b, CTX_LEN, GQA_WIDTH), BF16),
                   jax.ShapeDtypeStruct((n_b, CTX_LEN, MLA_WIDTH), BF16)],
        compiler_params=pltpu.CompilerParams(
            dimension_semantics=("arbitrary",), vmem_limit_bytes=VMEM_LIMIT),
        name="attn_ctx",
    )(qa, ka, va, qm, km, vm, g)


def _out_proj_kernel(*refs, layer, last, n_blk):
    refs, (w_ref, xbuf, abuf, mbuf, stage, sem, wsem) = refs[:-7], refs[-7:]
    if last:
        ya_hbm, ym_hbm, w_hbm, x_hbm, mod_ref, fg_ref, o_ref = refs
    else:
        ya_hbm, ym_hbm, yca_ref, ycm_ref, w_hbm, x_hbm, xc_ref, mod_ref, o_ref = refs

    def residual(a_ref, m_ref, x_ref):
        y = jnp.concatenate([a_ref[...], m_ref[...]], axis=-1)
        return x_ref[...] + mod_ref[2:3, :] * jnp.dot(y, w_ref[...], preferred_element_type=F32)

    j = pl.program_id(1)
    step = pl.program_id(0) * n_blk + j
    n_steps = pl.num_programs(0) * n_blk
    rings = ((x_hbm, xbuf), (ya_hbm, abuf), (ym_hbm, mbuf))

    def fetch(s):
        s = jnp.asarray(s, jnp.int32)
        row0 = pl.multiple_of((s % n_blk) * TM_OUT, TM_OUT)
        slot = s % X_SLOTS
        return [pltpu.make_async_copy(hbm.at[s // n_blk, pl.ds(row0, TM_OUT), :], buf.at[slot],
                                      sem.at[i, slot]) for i, (hbm, buf) in enumerate(rings)]

    def start_fetch(s):
        s = jnp.asarray(s, jnp.int32)

        @pl.when(jnp.logical_and(s < n_steps, s % n_blk < N_LAT_OUT))
        def _():
            for copy in fetch(s):
                copy.start()

    @pl.when(step == 0)
    def _():
        for s in range(X_SLOTS - 1):
            start_fetch(s)
        _load_weight(w_hbm, layer, w_ref, stage, wsem)

    start_fetch(step + (X_SLOTS - 1))

    @pl.when(j < N_LAT_OUT)
    def _():
        for copy in fetch(step):
            copy.wait()
        slot = step % X_SLOTS
        x_new = residual(abuf.at[slot], mbuf.at[slot], xbuf.at[slot])
        o_ref[...] = _rms(x_new) * fg_ref[...] if last else x_new

    if not last:
        @pl.when(j == N_LAT_OUT)
        def _():
            o_ref[:CTX_LEN, :] = residual(yca_ref, ycm_ref, xc_ref)
            o_ref[CTX_LEN:, :] = jnp.zeros((TM_OUT - CTX_LEN, D_MODEL), F32)


def _out_proj(layer, y_lat, y_ctx, w_out, stream, mod, final_g):
    n_b = y_lat[0].shape[0]
    last = y_ctx is None
    n_blk = N_LAT_OUT if last else N_LAT_OUT + 1
    ctx_blk = 0 if len(stream) == 2 else SEQ // CTX_LEN
    ctx = lambda width, blk: pl.BlockSpec((None, CTX_LEN, width), lambda b, j: (b, blk, 0))
    hbm = pl.BlockSpec(memory_space=pl.ANY)
    mod_spec = pl.BlockSpec((None, None, 3, D_MODEL),
                            lambda b, j: (layer, jnp.where(j == N_LAT_OUT, ADA_CTX_ROW, b), 0, 0))
    if last:
        in_specs = [hbm, hbm, hbm, hbm, mod_spec, pl.BlockSpec((1, D_MODEL), lambda b, j: (0, 0))]
        args = (*y_lat, w_out, stream[0], mod, final_g.reshape(1, D_MODEL))
    else:
        in_specs = [hbm, hbm, ctx(GQA_WIDTH, 0), ctx(MLA_WIDTH, 0), hbm,
                    hbm, ctx(D_MODEL, ctx_blk), mod_spec]
        args = (*y_lat, *y_ctx, w_out, stream[0], stream[-1], mod)
    return pl.pallas_call(
        functools.partial(_out_proj_kernel, layer=layer, last=last, n_blk=n_blk),
        grid=(n_b, n_blk),
        in_specs=in_specs,
        out_specs=pl.BlockSpec((None, TM_OUT, D_MODEL), lambda b, j: (b, j, 0)),
        out_shape=jax.ShapeDtypeStruct((n_b, n_blk * TM_OUT, D_MODEL), F32),
        scratch_shapes=[pltpu.VMEM((D_MIX, D_MODEL), BF16),
                        pltpu.VMEM((X_SLOTS, TM_OUT, D_MODEL), F32),
                        pltpu.VMEM((X_SLOTS, TM_OUT, GQA_WIDTH), BF16),
                        pltpu.VMEM((X_SLOTS, TM_OUT, MLA_WIDTH), BF16),
                        pltpu.VMEM((W_SLOTS, W_CHUNK, D_MODEL), F32),
                        pltpu.SemaphoreType.DMA((3, X_SLOTS)),
                        pltpu.SemaphoreType.DMA((W_SLOTS,))],
        compiler_params=pltpu.CompilerParams(
            dimension_semantics=("arbitrary", "arbitrary"), vmem_limit_bytes=VMEM_LIMIT),
        name="out_proj",
    )(*args)


def _rope_table():
    f32 = np.float32
    rows = SEQ // GRID_W
    row = np.repeat(np.arange(rows, dtype=f32), GRID_W)
    col = np.tile(np.arange(GRID_W, dtype=f32), rows)

    def cos_sin(rot_dim):
        n_freq = rot_dim // 4
        inv = (f32(ROPE_THETA) ** (-np.arange(n_freq, dtype=f32) / f32(n_freq))).astype(f32)
        ang = np.concatenate([row[:, None] * inv[None], col[:, None] * inv[None]], axis=-1)
        return np.cos(ang).astype(f32), np.sin(ang).astype(f32)

    cos, sin = cos_sin(HEAD_DIM)
    cos_m, sin_m = cos_sin(MLA_ROPE)
    z32 = np.zeros_like(sin_m)
    z64 = np.zeros_like(sin)
    lat = np.concatenate([
        cos, cos,
        -sin, sin,
        cos_m, cos_m, z64,
        -sin_m, z32, z64,
        z32, sin_m, z64,
    ], axis=-1)
    ctx_row = np.concatenate([np.ones(LANES, f32), np.zeros(LANES, f32),
                              np.ones(LANES, f32), np.zeros(2 * LANES, f32)])
    return jnp.asarray(np.concatenate([lat, np.broadcast_to(ctx_row, (CTX_LEN, TAB_WIDTH))], axis=0))


def kernel(x, c, ctx, c_ctx, w_ada, b_ada, norm_g, w_in, q_gain, k_gain, cq_gain, ckv_gain,
           w_uq, w_ukv, w_out, final_g):
    n_b = x.shape[0]
    assert x.shape == (n_b, SEQ, D_MODEL) and ctx.shape == (n_b, CTX_LEN, D_MODEL)
    assert n_b == ADA_CTX_ROW and w_ada.shape[0] == DEPTH and w_in.shape[-1] == D_IN

    c_all = jnp.concatenate(
        [c, c_ctx[None], jnp.zeros((ADA_ROWS - n_b - 1, D_MODEL), F32)], axis=0)
    mod = _ada_rows(c_all, w_ada, b_ada).reshape(DEPTH, ADA_ROWS, 3, D_MODEL)

    row3 = lambda a: a.reshape(DEPTH, 1, a.shape[-1])
    params = (row3(norm_g), jnp.swapaxes(w_in, 1, 2), row3(q_gain), row3(k_gain), row3(cq_gain),
              row3(ckv_gain), w_uq, w_ukv)
    tab = _rope_table()

    stream = (x, ctx)
    for layer in range(DEPTH):
        last = layer == DEPTH - 1
        qa, ka, va, qm, km, vm, g = _in_proj(layer, stream, mod, params, tab)
        y_lat = (_attn_gqa(qa, ka, va, g), _attn_mla(qm, km, vm, g))
        y_ctx = None if last else _attn_ctx(qa, ka, va, qm, km, vm, g)
        stream = (_out_proj(layer, y_lat, y_ctx, w_out, stream, mod, final_g),)
    return stream[0]
```

```python
import functools

import jax
import jax.numpy as jnp
import numpy as np
from jax import lax
from jax.experimental import pallas as pl
from jax.experimental.pallas import tpu as pltpu

F32 = jnp.float32
BF16 = jnp.bfloat16

D_MODEL = 2048
SEQ = 2048
CTX_LEN = 256
T_TOK = SEQ + CTX_LEN
GRID_W = 64
ROPE_THETA = 10000.0
EPS = 1e-6
DEPTH = 4
LOG2_E = 1.4426950408889634

LANES = 128
HEAD_DIM = 128
GQA_HEADS = 8
GQA_KV_HEADS = 2
GQA_GROUP = GQA_HEADS // GQA_KV_HEADS
GQA_WIDTH = GQA_HEADS * HEAD_DIM
GQA_KV_WIDTH = GQA_KV_HEADS * HEAD_DIM

MLA_HEADS = 8
MLA_NOPE = 128
MLA_ROPE = 64
MLA_QK = MLA_NOPE + MLA_ROPE
MLA_V = 128
MLA_Q_RANK = 512
MLA_KV_RANK = 512
MLA_WIDTH = MLA_HEADS * MLA_V
D_MIX = GQA_WIDTH + MLA_WIDTH
MLA_DK = MLA_NOPE + LANES
MLA_QK_WIDTH = MLA_HEADS * MLA_DK
V_EXT = 2 * HEAD_DIM

COL_Q = 0
COL_K = COL_Q + GQA_WIDTH
COL_V = COL_K + GQA_KV_WIDTH
COL_GG = COL_V + GQA_KV_WIDTH
COL_CQ = COL_GG + GQA_WIDTH
COL_CKV = COL_CQ + MLA_Q_RANK
COL_KR = COL_CKV + MLA_KV_RANK
COL_GM = COL_KR + MLA_ROPE
D_IN = COL_GM + MLA_WIDTH

TAB_COS_A, TAB_SIN_A, TAB_COS_M, TAB_SIN_M_LO, TAB_SIN_M_HI = range(5)
TAB_WIDTH = 5 * LANES

ADA_ROWS = 8
ADA_CTX_ROW = 4
ADA_TN = 1024

TM = 256
TQ_GQA = 512
TQ_MLA = 1024
N_BLK = T_TOK // TM
N_LAT_BLK = SEQ // TM
TM_OUT = 512
N_LAT_OUT = SEQ // TM_OUT
X_SLOTS = 3
W_CHUNK = 64
W_SLOTS = 8
MLA_HEADS_PER_STEP = 4

VMEM_LIMIT = 56 * 1024 * 1024

assert CTX_LEN == TM and SEQ % TQ_GQA == 0 and SEQ % TQ_MLA == 0


def _silu(x):
    return x / (1.0 + jnp.exp(-x))


def _rms(x):
    return x * lax.rsqrt(jnp.mean(x * x, axis=-1, keepdims=True) + EPS)


def _tab(tab_ref, col):
    return tab_ref[:, col * LANES:(col + 1) * LANES]


def _rot_a(x, tab_ref):
    return x * _tab(tab_ref, TAB_COS_A) + pltpu.roll(x, HEAD_DIM // 2, 1) * _tab(tab_ref, TAB_SIN_A)


def _rot_m(x, tab_ref, scale=None):
    half = MLA_ROPE // 2
    y = (x * _tab(tab_ref, TAB_COS_M)
         + pltpu.roll(x, LANES - half, 1) * _tab(tab_ref, TAB_SIN_M_LO)
         + pltpu.roll(x, half, 1) * _tab(tab_ref, TAB_SIN_M_HI))
    return y if scale is None else y * scale


def _mod_row(b, j):
    return jnp.where(j == N_LAT_BLK, ADA_CTX_ROW, b)


def _stream_block(xl_ref, xc_ref):
    return jnp.where(pl.program_id(1) == N_LAT_BLK, xc_ref[...], xl_ref[...])


def _ada_kernel(c_ref, w_ref, b_ref, o_ref):
    s = _silu(c_ref[...]).astype(BF16)
    w = w_ref[...].astype(BF16)
    o_ref[...] = jnp.dot(s, w, preferred_element_type=F32) + b_ref[...]


def _ada_rows(c_all, w_ada, b_ada):
    n_out = 3 * D_MODEL
    return pl.pallas_call(
        _ada_kernel,
        grid=(DEPTH, n_out // ADA_TN),
        in_specs=[
            pl.BlockSpec((ADA_ROWS, D_MODEL), lambda l, n: (0, 0)),
            pl.BlockSpec((None, D_MODEL, ADA_TN), lambda l, n: (l, 0, n)),
            pl.BlockSpec((None, 1, ADA_TN), lambda l, n: (l, 0, n)),
        ],
        out_specs=pl.BlockSpec((None, ADA_ROWS, ADA_TN), lambda l, n: (l, 0, n)),
        out_shape=jax.ShapeDtypeStruct((DEPTH, ADA_ROWS, n_out), F32),
        compiler_params=pltpu.CompilerParams(
            dimension_semantics=("arbitrary", "arbitrary"), vmem_limit_bytes=VMEM_LIMIT),
        name="ada_rows",
    )(c_all, w_ada, b_ada.reshape(DEPTH, 1, n_out))


def _store_bf16(dst_ref, rows, src_ref):
    dst_ref[rows, :] = src_ref[...].astype(BF16)


def _store_w_uq(dst_ref, rows, src_ref):
    low_half = lax.broadcasted_iota(jnp.int32, (W_CHUNK, LANES), 1) < MLA_ROPE
    for i in range(MLA_HEADS):
        c0 = i * MLA_QK
        tail = src_ref[:, c0 + MLA_QK - LANES:c0 + MLA_QK]
        rope = jnp.where(low_half, pltpu.roll(tail, MLA_ROPE, 1), 0.0)
        dst_ref[rows, i * MLA_DK:i * MLA_DK + MLA_NOPE] = src_ref[:, c0:c0 + MLA_NOPE].astype(BF16)
        dst_ref[rows, i * MLA_DK + MLA_NOPE:(i + 1) * MLA_DK] = rope.astype(BF16)


def _load_weight(w_hbm, layer, dst_ref, stage, sem, store=_store_bf16):
    n_chunks = w_hbm.shape[1] // W_CHUNK

    def rows(c):
        row0 = c * W_CHUNK
        return pl.ds(row0 if isinstance(c, int) else pl.multiple_of(row0, W_CHUNK), W_CHUNK)

    def chunk(c):
        slot = c % W_SLOTS
        return pltpu.make_async_copy(w_hbm.at[layer, rows(c), :], stage.at[slot], sem.at[slot])

    for c in range(W_SLOTS):
        chunk(c).start()

    def convert(c, carry):
        chunk(c).wait()
        store(dst_ref, rows(c), stage.at[c % W_SLOTS])
        pl.when(c + W_SLOTS < n_chunks)(lambda: chunk(c + W_SLOTS).start())
        return carry

    lax.fori_loop(0, n_chunks, convert, 0)


def _first_step():
    return jnp.logical_and(pl.program_id(0) == 0, pl.program_id(1) == 0)


def _in_proj_kernel(*refs, layer, paired, last):
    x_refs, refs = (refs[:2], refs[2:]) if paired else (refs[:1], refs[1:])
    (mod_ref, ng_ref, w_in_hbm, qg_ref, kg_ref, cqg_ref, ckvg_ref, w_uq_hbm, w_ukv_hbm, tab_ref,
     qa_ref, ka_ref, va_ref, qm_ref, km_ref, vm_ref, g_ref,
     w_in_ref, w_uq_ref, w_ukv_ref, stage, stage_uq, sem) = refs

    @pl.when(_first_step())
    def _():
        _load_weight(w_in_hbm, layer, w_in_ref, stage, sem)
        _load_weight(w_ukv_hbm, layer, w_ukv_ref, stage, sem)
        _load_weight(w_uq_hbm, layer, w_uq_ref, stage_uq, sem, _store_w_uq)

    def project(keys_only):
        x = _stream_block(*x_refs) if paired else x_refs[0][...]
        h = _rms(x) * (ng_ref[...] * (1.0 + mod_ref[1:2, :])) + mod_ref[0:1, :]
        hb = h.astype(BF16)
        proj = lambda r0, r1: lax.dot_general(hb, w_in_ref[r0:r1, :], (((1,), (1,)), ((), ())),
                                              preferred_element_type=F32)
        c0 = COL_CKV if keys_only else COL_CQ
        c = proj(c0, COL_KR)
        ckv = (_rms(c[:, COL_CKV - c0:]) * ckvg_ref[...]).astype(BF16)

        p0 = COL_K if keys_only else COL_Q
        p = proj(p0, COL_GG)
        if keys_only:
            qa_ref[...] = jnp.zeros(qa_ref.shape, BF16)
        else:
            q_gain = qg_ref[...] * (HEAD_DIM ** -0.5 * LOG2_E)
            for i in range(GQA_HEADS):
                q0 = COL_Q + i * HEAD_DIM
                qh = _rms(p[:, q0:q0 + HEAD_DIM]) * q_gain
                qa_ref[:, i * HEAD_DIM:(i + 1) * HEAD_DIM] = _rot_a(qh, tab_ref).astype(BF16)
        k_gain = kg_ref[...]
        for i in range(GQA_KV_HEADS):
            k0 = COL_K - p0 + i * HEAD_DIM
            kh = _rms(p[:, k0:k0 + HEAD_DIM]) * k_gain
            ka_ref[:, i * HEAD_DIM:(i + 1) * HEAD_DIM] = _rot_a(kh, tab_ref).astype(BF16)
        ones = jnp.ones((TM, HEAD_DIM), BF16)
        for i in range(GQA_KV_HEADS):
            v0 = COL_V - p0 + i * HEAD_DIM
            va_ref[:, i * V_EXT:i * V_EXT + HEAD_DIM] = p[:, v0:v0 + HEAD_DIM].astype(BF16)
            va_ref[:, i * V_EXT + HEAD_DIM:(i + 1) * V_EXT] = ones

        if keys_only:
            g_ref[...] = jnp.zeros(g_ref.shape, BF16)
        else:
            g_ref[:, :GQA_WIDTH] = _silu(proj(COL_GG, COL_CQ)).astype(BF16)
            g_ref[:, GQA_WIDTH:] = _silu(proj(COL_GM, D_IN)).astype(BF16)
        kr_in = proj(COL_KR, COL_KR + LANES)
        kr_in = jnp.where(lax.broadcasted_iota(jnp.int32, kr_in.shape, 1) < MLA_ROPE, kr_in, 0.0)

        if keys_only:
            qm_ref[...] = jnp.zeros(qm_ref.shape, BF16)
        else:
            cq = (_rms(c[:, :MLA_Q_RANK]) * cqg_ref[...]).astype(BF16)
            qm = jnp.dot(cq, w_uq_ref[...], preferred_element_type=F32)
            m_scale = MLA_QK ** -0.5 * LOG2_E
            for i in range(MLA_HEADS):
                m0 = i * MLA_DK
                qm_ref[:, m0:m0 + MLA_NOPE] = (qm[:, m0:m0 + MLA_NOPE] * m_scale).astype(BF16)
                qm_ref[:, m0 + MLA_NOPE:m0 + MLA_DK] = _rot_m(
                    qm[:, m0 + MLA_NOPE:m0 + MLA_DK], tab_ref, m_scale).astype(BF16)

        kvm = jnp.dot(ckv, w_ukv_ref[...], preferred_element_type=F32)
        kr = _rot_m(kr_in, tab_ref).astype(BF16)
        for i in range(MLA_HEADS):
            u0 = i * (MLA_NOPE + MLA_V)
            km_ref[:, i * MLA_DK:i * MLA_DK + MLA_NOPE] = kvm[:, u0:u0 + MLA_NOPE].astype(BF16)
            km_ref[:, i * MLA_DK + MLA_NOPE:(i + 1) * MLA_DK] = kr
            vm_ref[:, i * V_EXT:i * V_EXT + MLA_V] = (
                kvm[:, u0 + MLA_NOPE:u0 + MLA_NOPE + MLA_V].astype(BF16))
            vm_ref[:, i * V_EXT + MLA_V:(i + 1) * V_EXT] = ones

    if last:
        is_ctx = pl.program_id(1) == N_LAT_BLK
        pl.when(jnp.logical_not(is_ctx))(lambda: project(False))
        pl.when(is_ctx)(lambda: project(True))
    else:
        project(False)


def _in_proj(layer, stream, mod, params, tab):
    n_b = stream[0].shape[0]
    paired = len(stream) == 2
    tok = lambda width: pl.BlockSpec((None, TM, width), lambda b, j: (b, j, 0))
    stream_specs = [
        pl.BlockSpec((None, TM, D_MODEL), lambda b, j: (b, jnp.minimum(j, N_LAT_BLK - 1), 0)),
        pl.BlockSpec((None, CTX_LEN, D_MODEL), lambda b, j: (b, 0, 0))] if paired else [tok(D_MODEL)]
    lay = lambda *shape: pl.BlockSpec((None,) + shape, lambda b, j: (layer,) + (0,) * len(shape))
    f32_hbm = pl.BlockSpec(memory_space=pl.ANY)
    widths = (GQA_WIDTH, GQA_KV_WIDTH, GQA_KV_HEADS * V_EXT,
              MLA_QK_WIDTH, MLA_QK_WIDTH, MLA_HEADS * V_EXT, D_MIX)
    return pl.pallas_call(
        functools.partial(_in_proj_kernel, layer=layer, paired=paired, last=layer == DEPTH - 1),
        grid=(n_b, N_BLK),
        in_specs=stream_specs + [
            pl.BlockSpec((None, None, 3, D_MODEL), lambda b, j: (layer, _mod_row(b, j), 0, 0)),
            lay(1, D_MODEL),
            f32_hbm,
            lay(1, HEAD_DIM), lay(1, HEAD_DIM), lay(1, MLA_Q_RANK), lay(1, MLA_KV_RANK),
            f32_hbm,
            f32_hbm,
            pl.BlockSpec((TM, TAB_WIDTH), lambda b, j: (j, 0)),
        ],
        out_specs=[tok(w) for w in widths],
        out_shape=[jax.ShapeDtypeStruct((n_b, T_TOK, w), BF16) for w in widths],
        scratch_shapes=[pltpu.VMEM((D_IN, D_MODEL), BF16),
                        pltpu.VMEM((MLA_Q_RANK, MLA_QK_WIDTH), BF16),
                        pltpu.VMEM((MLA_KV_RANK, MLA_HEADS * (MLA_NOPE + MLA_V)), BF16),
                        pltpu.VMEM((W_SLOTS, W_CHUNK, D_MODEL), F32),
                        pltpu.VMEM((W_SLOTS, W_CHUNK, MLA_HEADS * MLA_QK), F32),
                        pltpu.SemaphoreType.DMA((W_SLOTS,))],
        compiler_params=pltpu.CompilerParams(
            dimension_semantics=("arbitrary", "arbitrary"), vmem_limit_bytes=VMEM_LIMIT),
        name="in_proj",
    )(*stream, mod, *params, tab)


def _attend(q_ref, k_ref, v_ref, g_ref, o_ref, *, kv_heads, group, dk, dv):
    for p in range(kv_heads):
        k = k_ref[:, p * dk:(p + 1) * dk]
        v = v_ref[:, p * V_EXT:(p + 1) * V_EXT]
        for r0 in range(0, q_ref.shape[0], TM):
            for i in range(group):
                hd = p * group + i
                q = q_ref[r0:r0 + TM, hd * dk:(hd + 1) * dk]
                s = lax.dot_general(q, k, (((1,), (1,)), ((), ())), preferred_element_type=F32)
                e = jnp.exp2(s - jnp.max(s, axis=-1, keepdims=True))
                o = jnp.dot(e.astype(BF16), v, preferred_element_type=F32)
                gate = g_ref[r0:r0 + TM, hd * dv:(hd + 1) * dv].astype(F32)
                o_ref[r0:r0 + TM, hd * dv:(hd + 1) * dv] = (
                    o[:, :dv] * (gate / o[:, dv:])).astype(BF16)


_GQA_CFG = dict(kv_heads=GQA_KV_HEADS, group=GQA_GROUP, dk=HEAD_DIM, dv=HEAD_DIM)


def _attn_gqa(q, k, v, g):
    n_b = q.shape[0]
    return pl.pallas_call(
        functools.partial(_attend, **_GQA_CFG),
        grid=(n_b, SEQ // TQ_GQA),
        in_specs=[
            pl.BlockSpec((None, TQ_GQA, GQA_WIDTH), lambda b, j: (b, j, 0)),
            pl.BlockSpec((None, T_TOK, GQA_KV_WIDTH), lambda b, j: (b, 0, 0)),
            pl.BlockSpec((None, T_TOK, GQA_KV_HEADS * V_EXT), lambda b, j: (b, 0, 0)),
            pl.BlockSpec((None, TQ_GQA, GQA_WIDTH), lambda b, j: (b, j, 0)),
        ],
        out_specs=pl.BlockSpec((None, TQ_GQA, GQA_WIDTH), lambda b, j: (b, j, 0)),
        out_shape=jax.ShapeDtypeStruct((n_b, SEQ, GQA_WIDTH), BF16),
        compiler_params=pltpu.CompilerParams(
            dimension_semantics=("arbitrary", "arbitrary"), vmem_limit_bytes=VMEM_LIMIT),
        name="attn_gqa",
    )(q, k, v, g)


def _attn_mla(q, k, v, g):
    n_b = q.shape[0]
    hps = MLA_HEADS_PER_STEP
    g_col0 = GQA_WIDTH // (hps * MLA_V)
    return pl.pallas_call(
        functools.partial(_attend, kv_heads=hps, group=1, dk=MLA_DK, dv=MLA_V),
        grid=(n_b, MLA_HEADS // hps, SEQ // TQ_MLA),
        in_specs=[
            pl.BlockSpec((None, TQ_MLA, hps * MLA_DK), lambda b, hg, j: (b, j, hg)),
            pl.BlockSpec((None, T_TOK, hps * MLA_DK), lambda b, hg, j: (b, 0, hg)),
            pl.BlockSpec((None, T_TOK, hps * V_EXT), lambda b, hg, j: (b, 0, hg)),
            pl.BlockSpec((None, TQ_MLA, hps * MLA_V), lambda b, hg, j: (b, j, g_col0 + hg)),
        ],
        out_specs=pl.BlockSpec((None, TQ_MLA, hps * MLA_V), lambda b, hg, j: (b, j, hg)),
        out_shape=jax.ShapeDtypeStruct((n_b, SEQ, MLA_WIDTH), BF16),
        compiler_params=pltpu.CompilerParams(
            dimension_semantics=("arbitrary", "arbitrary", "arbitrary"), vmem_limit_bytes=VMEM_LIMIT),
        name="attn_mla",
    )(q, k, v, g)


def _attn_ctx_kernel(qa_ref, ka_ref, va_ref, qm_ref, km_ref, vm_ref, g_ref, oa_ref, om_ref):
    _attend(qa_ref, ka_ref, va_ref, g_ref.at[:, :GQA_WIDTH], oa_ref, **_GQA_CFG)
    _attend(qm_ref, km_ref, vm_ref, g_ref.at[:, GQA_WIDTH:], om_ref,
            kv_heads=MLA_HEADS, group=1, dk=MLA_DK, dv=MLA_V)


def _attn_ctx(qa, ka, va, qm, km, vm, g):
    n_b = qa.shape[0]
    ctx_rows = lambda width: pl.BlockSpec((None, CTX_LEN, width), lambda b: (b, SEQ // CTX_LEN, 0))
    out = lambda width: pl.BlockSpec((None, CTX_LEN, width), lambda b: (b, 0, 0))
    return pl.pallas_call(
        _attn_ctx_kernel,
        grid=(n_b,),
        in_specs=[ctx_rows(GQA_WIDTH), ctx_rows(GQA_KV_WIDTH), ctx_rows(GQA_KV_HEADS * V_EXT),
                  ctx_rows(MLA_QK_WIDTH), ctx_rows(MLA_QK_WIDTH), ctx_rows(MLA_HEADS * V_EXT),
                  ctx_rows(D_MIX)],
        out_specs=[out(GQA_WIDTH), out(MLA_WIDTH)],
        out_shape=[jax.ShapeDtypeStruct((n_b, CTX_LEN, GQA_WIDTH), BF16),
                   jax.ShapeDtypeStruct((n_b, CTX_LEN, MLA_WIDTH), BF16)],
        compiler_params=pltpu.CompilerParams(
            dimension_semantics=("arbitrary",), vmem_limit_bytes=VMEM_LIMIT),
        name="attn_ctx",
    )(qa, ka, va, qm, km, vm, g)


def _out_proj_kernel(*refs, layer, last, n_blk):
    refs, (w_ref, xbuf, abuf, mbuf, stage, sem, wsem) = refs[:-7], refs[-7:]
    if last:
        ya_hbm, ym_hbm, w_hbm, x_hbm, mod_ref, fg_ref, o_ref = refs
    else:
        ya_hbm, ym_hbm, yca_ref, ycm_ref, w_hbm, x_hbm, xc_ref, mod_ref, o_ref = refs

    def residual(a_ref, m_ref, x_ref):
        y = jnp.concatenate([a_ref[...], m_ref[...]], axis=-1)
        return x_ref[...] + mod_ref[2:3, :] * jnp.dot(y, w_ref[...], preferred_element_type=F32)

    j = pl.program_id(1)
    step = pl.program_id(0) * n_blk + j
    n_steps = pl.num_programs(0) * n_blk
    rings = ((x_hbm, xbuf), (ya_hbm, abuf), (ym_hbm, mbuf))

    def fetch(s):
        s = jnp.asarray(s, jnp.int32)
        row0 = pl.multiple_of((s % n_blk) * TM_OUT, TM_OUT)
        slot = s % X_SLOTS
        return [pltpu.make_async_copy(hbm.at[s // n_blk, pl.ds(row0, TM_OUT), :], buf.at[slot],
                                      sem.at[i, slot]) for i, (hbm, buf) in enumerate(rings)]

    def start_fetch(s):
        s = jnp.asarray(s, jnp.int32)

        @pl.when(jnp.logical_and(s < n_steps, s % n_blk < N_LAT_OUT))
        def _():
            for copy in fetch(s):
                copy.start()

    @pl.when(step == 0)
    def _():
        for s in range(X_SLOTS - 1):
            start_fetch(s)
        _load_weight(w_hbm, layer, w_ref, stage, wsem)

    start_fetch(step + (X_SLOTS - 1))

    @pl.when(j < N_LAT_OUT)
    def _():
        for copy in fetch(step):
            copy.wait()
        slot = step % X_SLOTS
        x_new = residual(abuf.at[slot], mbuf.at[slot], xbuf.at[slot])
        o_ref[...] = _rms(x_new) * fg_ref[...] if last else x_new

    if not last:
        @pl.when(j == N_LAT_OUT)
        def _():
            o_ref[:CTX_LEN, :] = residual(yca_ref, ycm_ref, xc_ref)
            o_ref[CTX_LEN:, :] = jnp.zeros((TM_OUT - CTX_LEN, D_MODEL), F32)


def _out_proj(layer, y_lat, y_ctx, w_out, stream, mod, final_g):
    n_b = y_lat[0].shape[0]
    last = y_ctx is None
    n_blk = N_LAT_OUT if last else N_LAT_OUT + 1
    ctx_blk = 0 if len(stream) == 2 else SEQ // CTX_LEN
    ctx = lambda width, blk: pl.BlockSpec((None, CTX_LEN, width), lambda b, j: (b, blk, 0))
    hbm = pl.BlockSpec(memory_space=pl.ANY)
    mod_spec = pl.BlockSpec((None, None, 3, D_MODEL),
                            lambda b, j: (layer, jnp.where(j == N_LAT_OUT, ADA_CTX_ROW, b), 0, 0))
    if last:
        in_specs = [hbm, hbm, hbm, hbm, mod_spec, pl.BlockSpec((1, D_MODEL), lambda b, j: (0, 0))]
        args = (*y_lat, w_out, stream[0], mod, final_g.reshape(1, D_MODEL))
    else:
        in_specs = [hbm, hbm, ctx(GQA_WIDTH, 0), ctx(MLA_WIDTH, 0), hbm,
                    hbm, ctx(D_MODEL, ctx_blk), mod_spec]
        args = (*y_lat, *y_ctx, w_out, stream[0], stream[-1], mod)
    return pl.pallas_call(
        functools.partial(_out_proj_kernel, layer=layer, last=last, n_blk=n_blk),
        grid=(n_b, n_blk),
        in_specs=in_specs,
        out_specs=pl.BlockSpec((None, TM_OUT, D_MODEL), lambda b, j: (b, j, 0)),
        out_shape=jax.ShapeDtypeStruct((n_b, n_blk * TM_OUT, D_MODEL), F32),
        scratch_shapes=[pltpu.VMEM((D_MIX, D_MODEL), BF16),
                        pltpu.VMEM((X_SLOTS, TM_OUT, D_MODEL), F32),
                        pltpu.VMEM((X_SLOTS, TM_OUT, GQA_WIDTH), BF16),
                        pltpu.VMEM((X_SLOTS, TM_OUT, MLA_WIDTH), BF16),
                        pltpu.VMEM((W_SLOTS, W_CHUNK, D_MODEL), F32),
                        pltpu.SemaphoreType.DMA((3, X_SLOTS)),
                        pltpu.SemaphoreType.DMA((W_SLOTS,))],
        compiler_params=pltpu.CompilerParams(
            dimension_semantics=("arbitrary", "arbitrary"), vmem_limit_bytes=VMEM_LIMIT),
        name="out_proj",
    )(*args)


def _rope_table():
    f32 = np.float32
    rows = SEQ // GRID_W
    row = np.repeat(np.arange(rows, dtype=f32), GRID_W)
    col = np.tile(np.arange(GRID_W, dtype=f32), rows)

    def cos_sin(rot_dim):
        n_freq = rot_dim // 4
        inv = (f32(ROPE_THETA) ** (-np.arange(n_freq, dtype=f32) / f32(n_freq))).astype(f32)
        ang = np.concatenate([row[:, None] * inv[None], col[:, None] * inv[None]], axis=-1)
        return np.cos(ang).astype(f32), np.sin(ang).astype(f32)

    cos, sin = cos_sin(HEAD_DIM)
    cos_m, sin_m = cos_sin(MLA_ROPE)
    z32 = np.zeros_like(sin_m)
    z64 = np.zeros_like(sin)
    lat = np.concatenate([
        cos, cos,
        -sin, sin,
        cos_m, cos_m, z64,
        -sin_m, z32, z64,
        z32, sin_m, z64,
    ], axis=-1)
    ctx_row = np.concatenate([np.ones(LANES, f32), np.zeros(LANES, f32),
                              np.ones(LANES, f32), np.zeros(2 * LANES, f32)])
    return jnp.asarray(np.concatenate([lat, np.broadcast_to(ctx_row, (CTX_LEN, TAB_WIDTH))], axis=0))


def kernel(x, c, ctx, c_ctx, w_ada, b_ada, norm_g, w_in, q_gain, k_gain, cq_gain, ckv_gain,
           w_uq, w_ukv, w_out, final_g):
    n_b = x.shape[0]
    assert x.shape == (n_b, SEQ, D_MODEL) and ctx.shape == (n_b, CTX_LEN, D_MODEL)
    assert n_b == ADA_CTX_ROW and w_ada.shape[0] == DEPTH and w_in.shape[-1] == D_IN

    c_all = jnp.concatenate(
        [c, c_ctx[None], jnp.zeros((ADA_ROWS - n_b - 1, D_MODEL), F32)], axis=0)
    mod = _ada_rows(c_all, w_ada, b_ada).reshape(DEPTH, ADA_ROWS, 3, D_MODEL)

    row3 = lambda a: a.reshape(DEPTH, 1, a.shape[-1])
    params = (row3(norm_g), jnp.swapaxes(w_in, 1, 2), row3(q_gain), row3(k_gain), row3(cq_gain),
              row3(ckv_gain), w_uq, w_ukv)
    tab = _rope_table()

    stream = (x, ctx)
    for layer in range(DEPTH):
        last = layer == DEPTH - 1
        qa, ka, va, qm, km, vm, g = _in_proj(layer, stream, mod, params, tab)
        y_lat = (_attn_gqa(qa, ka, va, g), _attn_mla(qm, km, vm, g))
        y_ctx = None if last else _attn_ctx(qa, ka, va, qm, km, vm, g)
        stream = (_out_proj(layer, y_lat, y_ctx, w_out, stream, mod, final_g),)
    return stream[0]
```
